```python
import jax, jax.numpy as jnp
from jax import lax
import numpy as np

D_MODEL = 1024
BATCH = 32
SEQ = 2048
DEPTH = 4
DEC_BATCH = 16
DEC_SEQ = 4096
PAST_LEN = 128

HG_HEADS = 4
HG_DK = D_MODEL // (2 * HG_HEADS)
HG_DV = HG_DK
HG_WIDTH = HG_HEADS * HG_DK
GLA_HEADS = 4
GLA_WIDTH = D_MODEL - HG_WIDTH
GLA_DV = GLA_WIDTH // GLA_HEADS
GLA_DK = GLA_DV // 2
GLA_QK = GLA_HEADS * GLA_DK
GLA_GATE_RANK = 16
GLA_GATE_NORMALIZER = 16.0
IN_WIDTH = 5 * HG_WIDTH + 2 * GLA_QK + 2 * GLA_WIDTH + 2 * GLA_GATE_RANK
CHUNK = 64
D_FF = 2816
N_EXPERTS = 8
TOP_K = 2
D_FF_EXPERT = 3584
MOE_BLOCK = 256
N_DENSE = (DEPTH + 1) // 2
N_MOE = DEPTH // 2
EPS = 1e-6

kernel_name = "hybrid_bidir_hgrn2_gla_adaln_moe_encoder"


def _split_points():
    sizes = (HG_WIDTH,) * 5 + (GLA_QK, GLA_QK, GLA_WIDTH, GLA_WIDTH, GLA_GATE_RANK, GLA_GATE_RANK)
    pts, acc = [], 0
    for s in sizes[:-1]:
        acc += s
        pts.append(acc)
    return pts


def rms_norm(x, w):
    xf = x.astype(jnp.float32)
    y = xf * lax.rsqrt(jnp.mean(xf * xf, axis=-1, keepdims=True) + EPS)
    return (y * w.astype(jnp.float32)).astype(x.dtype)


def chunk_gated_linear(q, k, v, log_g):
    B, T, H, dk = q.shape
    dv = v.shape[-1]
    n = T // CHUNK
    blk = lambda a: a.reshape(B, n, CHUNK, H, a.shape[-1])
    q, k, v = blk(q), blk(k), blk(v)
    b = jnp.cumsum(blk(log_g), axis=2)
    b_ref = b[:, :, CHUNK // 2:CHUNK // 2 + 1]
    b_last = b[:, :, -1]
    scores = jnp.einsum('bnthd,bnshd->bnhts', q * jnp.exp(b - b_ref), k * jnp.exp(b_ref - b))
    lower = jnp.tril(jnp.ones((CHUNK, CHUNK), dtype=bool))
    scores = jnp.where(lower, scores, 0.0)
    o = jnp.einsum('bnhts,bnshe->bnthe', scores, v)
    dS = jnp.einsum('bnshd,bnshe->nbhde', k * jnp.exp(b_last[:, :, None] - b), v)
    decay = jnp.moveaxis(jnp.exp(b_last), 1, 0)

    def step(S, inp):
        dS_c, dec_c = inp
        return dec_c[..., None] * S + dS_c, S

    _, S_prev = lax.scan(step, jnp.zeros((B, H, dk, dv), q.dtype), (dS, decay))
    o = o + jnp.einsum('bnthd,nbhde->bnthe', q * jnp.exp(b), S_prev)
    return o.reshape(B, T, H, dv)


def bidirectional(q, k_fwd, k_bwd, v, lg_fwd, lg_bwd):
    rev = lambda a: jnp.flip(a, axis=1)
    fwd = chunk_gated_linear(q, k_fwd, v, lg_fwd)
    bwd = rev(chunk_gated_linear(rev(q), rev(k_bwd), rev(v), rev(lg_bwd)))
    return fwd + bwd


def token_mixer(h, w_in, lb, w_gk_up, b_gk, hg_norm_w, gla_norm_w, w_out):
    B, T, _ = h.shape
    f32 = jnp.float32
    proj = h @ w_in
    (hq, hf_f, hf_b, hi, hg, gq, gk, gv, gg, glr_f, glr_b) = jnp.split(proj, _split_points(), axis=-1)
    heads = lambda a, nh: a.reshape(B, T, nh, -1)

    q_h = heads(jax.nn.silu(hq.astype(f32)), HG_HEADS) * (HG_DK ** -0.5)
    i_h = heads(hi.astype(f32), HG_HEADS)

    def hgrn_gate(z, lb_dir):
        z = z.astype(f32)
        log_f = jnp.logaddexp(jnp.log(lb_dir), jnp.log1p(-lb_dir) + jax.nn.log_sigmoid(z))
        one_minus_f = (1.0 - lb_dir) * jax.nn.sigmoid(-z)
        return heads(one_minus_f, HG_HEADS), heads(log_f, HG_HEADS)

    k_hf, lf_f = hgrn_gate(hf_f, lb[0])
    k_hb, lf_b = hgrn_gate(hf_b, lb[1])
    o_hg = bidirectional(q_h, k_hf, k_hb, i_h, lf_f, lf_b)

    q_g = heads(gq.astype(f32), GLA_HEADS) * (GLA_DK ** -0.5)
    k_g = heads(gk.astype(f32), GLA_HEADS)
    v_g = heads(gv.astype(f32), GLA_HEADS)

    def gla_gate(lr, d):
        z = lr.astype(f32) @ w_gk_up[d].astype(f32) + b_gk[d].astype(f32)
        return heads(jax.nn.log_sigmoid(z) / GLA_GATE_NORMALIZER, GLA_HEADS)

    o_gla = bidirectional(q_g, k_g, k_g, v_g, gla_gate(glr_f, 0), gla_gate(glr_b, 1))

    def gate_out(o, g, w, nh):
        return (rms_norm(o, w) * heads(jax.nn.silu(g.astype(f32)), nh)).reshape(B, T, -1)

    mixed = jnp.concatenate([gate_out(o_hg, hg, hg_norm_w, HG_HEADS),
                             gate_out(o_gla, gg, gla_norm_w, GLA_HEADS)], axis=-1).astype(h.dtype)
    return mixed @ w_out


def dense_swiglu(h, w1, w3, w2):
    return (jax.nn.silu(h @ w1) * (h @ w3)) @ w2


def moe_swiglu(h, w_router, b_router, w_e1, w_e3, w_e2):
    B, T, D = h.shape
    n_tok = B * T
    n_asg = n_tok * TOP_K
    xt = h.reshape(n_tok, D)
    logits = (xt @ w_router).astype(jnp.float32) + b_router.astype(jnp.float32)
    top_logit, top_idx = lax.top_k(logits, TOP_K)
    gates = jax.nn.softmax(top_logit, axis=-1)
    flat_e = top_idx.reshape(-1)
    flat_tok = jnp.repeat(jnp.arange(n_tok, dtype=jnp.int32), TOP_K)
    flat_gate = gates.reshape(-1)
    order = jnp.argsort(flat_e)
    se, stok, sgate = flat_e[order], flat_tok[order], flat_gate[order]
    counts = jnp.bincount(flat_e, length=N_EXPERTS)
    start = jnp.cumsum(counts) - counts
    pcounts = (counts + MOE_BLOCK - 1) // MOE_BLOCK * MOE_BLOCK
    pend = jnp.cumsum(pcounts)
    pstart = pend - pcounts
    dest = pstart[se] + jnp.arange(n_asg, dtype=jnp.int32) - start[se]
    n_blocks = -(-n_asg // MOE_BLOCK) + N_EXPERTS
    xs = jnp.zeros((n_blocks * MOE_BLOCK, D), h.dtype).at[dest].set(xt[stok])
    block_e = jnp.minimum(jnp.searchsorted(pend, jnp.arange(n_blocks, dtype=pend.dtype) * MOE_BLOCK,
                                           side='right'), N_EXPERTS - 1)

    def expert_block(args):
        xb, e = args
        return (jax.nn.silu(xb @ w_e1[e]) * (xb @ w_e3[e])) @ w_e2[e]

    ys = lax.map(expert_block, (xs.reshape(n_blocks, MOE_BLOCK, D), block_e)).reshape(-1, D)
    contrib = ys[dest] * sgate[:, None].astype(ys.dtype)
    out = jnp.zeros((n_tok, D), ys.dtype).at[stok].add(contrib)
    return out.reshape(B, T, D)


def trunk(x, c, w_ada, b_ada, norm1_w, w_in, hg_lb_logits, gla_w_gk_up, gla_b_gk, hg_norm_w,
          gla_norm_w, w_out, norm2_w, w_ff1, w_ff3, w_ff2, w_router, b_router, w_e1, w_e3, w_e2,
          final_norm_w):
    lb_all = jnp.cumsum(jax.nn.softmax(hg_lb_logits.astype(jnp.float32), axis=0), axis=0)
    lb_all = lb_all - lb_all[0]
    mods = jnp.einsum('bd,lde->lbe', jax.nn.silu(c), w_ada) + b_ada[:, None, :]
    for l in range(DEPTH):
        sh1, sc1, g1, sh2, sc2, g2 = jnp.split(mods[l][:, None, :], 6, axis=-1)
        h = rms_norm(x, norm1_w[l]) * (1 + sc1) + sh1
        x = x + g1 * token_mixer(h, w_in[l], lb_all[l], gla_w_gk_up[l], gla_b_gk[l],
                                 hg_norm_w[l], gla_norm_w[l], w_out[l])
        h = rms_norm(x, norm2_w[l]) * (1 + sc2) + sh2
        if l % 2 == 0:
            m = l // 2
            f = dense_swiglu(h, w_ff1[m], w_ff3[m], w_ff2[m])
        else:
            m = l // 2
            f = moe_swiglu(h, w_router[m], b_router[m], w_e1[m], w_e3[m], w_e2[m])
        x = x + g2 * f
    return rms_norm(x, final_norm_w)


def setup_inputs(seed: int = 0) -> dict:
    key = jax.random.key(seed)
    ks = jax.random.split(key, 24)
    n = lambda k, shape, s: jax.random.normal(k, shape, jnp.float32) * s
    D = D_MODEL
    return {
        "x_prompt": n(ks[0], (BATCH, SEQ, D), 1.0),
        "x_sample": n(ks[1], (DEC_BATCH, DEC_SEQ, D), 1.0),
        "c_prompt": n(ks[2], (BATCH, D), 1.0),
        "c_sample": n(ks[3], (DEC_BATCH, D), 1.0),
        "w_ada": n(ks[4], (DEPTH, D, 6 * D), 0.5 * D ** -0.5),
        "b_ada": n(ks[5], (DEPTH, 6 * D), 0.02),
        "norm1_w": 1.0 + n(ks[6], (DEPTH, D), 0.02),
        "w_in": n(ks[7], (DEPTH, D, IN_WIDTH), D ** -0.5),
        "hg_lb_logits": n(ks[8], (DEPTH, 2, HG_WIDTH), 0.5),
        "gla_w_gk_up": n(ks[9], (DEPTH, 2, GLA_GATE_RANK, GLA_QK), GLA_GATE_RANK ** -0.5),
        "gla_b_gk": n(ks[10], (DEPTH, 2, GLA_QK), 0.1),
        "hg_norm_w": 1.0 + n(ks[11], (DEPTH, HG_DV), 0.02),
        "gla_norm_w": 1.0 + n(ks[12], (DEPTH, GLA_DV), 0.02),
        "w_out": n(ks[13], (DEPTH, D, D), D ** -0.5),
        "norm2_w": 1.0 + n(ks[14], (DEPTH, D), 0.02),
        "w_ff1": n(ks[15], (N_DENSE, D, D_FF), D ** -0.5),
        "w_ff3": n(ks[16], (N_DENSE, D, D_FF), D ** -0.5),
        "w_ff2": n(ks[17], (N_DENSE, D_FF, D), D_FF ** -0.5),
        "w_router": n(ks[18], (N_MOE, D, N_EXPERTS), D ** -0.5),
        "b_router": n(ks[19], (N_MOE, N_EXPERTS), 0.01),
        "w_e1": n(ks[20], (N_MOE, N_EXPERTS, D, D_FF_EXPERT), D ** -0.5),
        "w_e3": n(ks[21], (N_MOE, N_EXPERTS, D, D_FF_EXPERT), D ** -0.5),
        "w_e2": n(ks[22], (N_MOE, N_EXPERTS, D_FF_EXPERT, D), D_FF_EXPERT ** -0.5),
        "final_norm_w": 1.0 + n(ks[23], (D,), 0.02),
    }


def reference(x_prompt, x_sample, c_prompt, c_sample, w_ada, b_ada, norm1_w, w_in, hg_lb_logits,
              gla_w_gk_up, gla_b_gk, hg_norm_w, gla_norm_w, w_out, norm2_w, w_ff1, w_ff3, w_ff2,
              w_router, b_router, w_e1, w_e3, w_e2, final_norm_w):
    y_prompt = trunk(x_prompt, c_prompt, w_ada, b_ada, norm1_w, w_in, hg_lb_logits, gla_w_gk_up,
                     gla_b_gk, hg_norm_w, gla_norm_w, w_out, norm2_w, w_ff1, w_ff3, w_ff2,
                     w_router, b_router, w_e1, w_e3, w_e2, final_norm_w)
    y_sample = trunk(x_sample, c_sample, w_ada, b_ada, norm1_w, w_in, hg_lb_logits, gla_w_gk_up,
                     gla_b_gk, hg_norm_w, gla_norm_w, w_out, norm2_w, w_ff1, w_ff3, w_ff2,
                     w_router, b_router, w_e1, w_e3, w_e2, final_norm_w)
    return (y_prompt, y_sample)
```

```python
import functools

import jax
import jax.numpy as jnp
from jax import lax
from jax.experimental import pallas as pl
from jax.experimental.pallas import tpu as pltpu

F32 = jnp.float32
BF16 = jnp.bfloat16

D_MODEL = 1024
HG_HEADS = 4
HG_DK = 128
HG_WIDTH = 512
GLA_HEADS = 4
GLA_DK = 64
GLA_DV = 128
GLA_QK = 256
GLA_WIDTH = 512
GLA_GATE_RANK = 16
GLA_GATE_NORMALIZER = 16.0
CHUNK = 64
N_EXPERTS = 8
TOP_K = 2
EPS = 1e-6
LANES = 128

PA_WIDTH = 3 * HG_WIDTH + 2 * GLA_QK + 2 * GLA_WIDTH
PG_WIDTH = 2 * HG_WIDTH + LANES
GATE_DTYPE = F32

VMEM_LIMIT = 48 * 1024 * 1024


def _cparams(*sem):
    return pltpu.CompilerParams(dimension_semantics=sem, vmem_limit_bytes=VMEM_LIMIT)


def _sigmoid(x):
    return 1.0 / (1.0 + jnp.exp(-x))


def _split_bf16(a):
    hi = a.astype(BF16)
    lo = (a - hi.astype(F32)).astype(BF16)
    return hi, lo


def _dot(a, b):
    return jnp.dot(a, b, preferred_element_type=F32)


def _dot_nt(a, b):
    return lax.dot_general(a, b, (((1,), (1,)), ((), ())), preferred_element_type=F32)


def _dot_tn(a, b):
    return lax.dot_general(a, b, (((0,), (0,)), ((), ())), preferred_element_type=F32)


def _dot3(a, b_hi, b_lo):
    a_hi, a_lo = _split_bf16(a)
    return _dot(a_hi, b_hi) + (_dot(a_hi, b_lo) + _dot(a_lo, b_hi))


def _rms(x, w):
    ms = jnp.mean(x * x, axis=-1, keepdims=True)
    return x * lax.rsqrt(ms + EPS) * w


def _ada_kernel(c_ref, w_ref, b_ref, o_ref):
    c = c_ref[...]
    s = c * _sigmoid(c)
    w_hi, w_lo = _split_bf16(w_ref[0])
    o_ref[0] = _dot3(s, w_hi, w_lo) + b_ref[0]


def _ada_mods(c, w_ada, b_ada):
    depth, d, e = w_ada.shape
    nb = c.shape[0]
    tn = 1536
    return pl.pallas_call(
        _ada_kernel,
        grid=(depth, e // tn),
        in_specs=[
            pl.BlockSpec((nb, d), lambda l, j: (0, 0)),
            pl.BlockSpec((1, d, tn), lambda l, j: (l, 0, j)),
            pl.BlockSpec((1, 1, tn), lambda l, j: (l, 0, j)),
        ],
        out_specs=pl.BlockSpec((1, nb, tn), lambda l, j: (l, 0, j)),
        out_shape=jax.ShapeDtypeStruct((depth, nb, e), F32),
        compiler_params=_cparams("arbitrary", "arbitrary"),
        name="ada_mods",
    )(c, w_ada, b_ada.reshape(depth, 1, e))


def _in_proj_kernel(x_ref, mod_ref, nw_ref, w_ref, pa_ref, pg_ref):
    m = mod_ref[0]
    h = _rms(x_ref[...], nw_ref[...]) * (1.0 + m[1:2, :]) + m[0:1, :]
    hb = h.astype(BF16)
    cw = 512
    for c in range(PA_WIDTH // cw):
        pa_ref[:, c * cw:(c + 1) * cw] = _dot(hb, w_ref[:, c * cw:(c + 1) * cw]).astype(BF16)
    for lo, hi in ((0, 512), (512, 1024), (1024, PG_WIDTH)):
        pg_ref[:, lo:hi] = _dot(hb, w_ref[:, PA_WIDTH + lo:PA_WIDTH + hi]).astype(pg_ref.dtype)


def _in_proj(x, mod, nw, w, seq_len, tm):
    n, d = x.shape
    tps = seq_len // tm
    return pl.pallas_call(
        _in_proj_kernel,
        grid=(n // tm,),
        in_specs=[
            pl.BlockSpec((tm, d), lambda i: (i, 0)),
            pl.BlockSpec((1, 6, d), lambda i: (i // tps, 0, 0)),
            pl.BlockSpec((1, d), lambda i: (0, 0)),
            pl.BlockSpec((d, PA_WIDTH + PG_WIDTH), lambda i: (0, 0)),
        ],
        out_specs=[
            pl.BlockSpec((tm, PA_WIDTH), lambda i: (i, 0)),
            pl.BlockSpec((tm, PG_WIDTH), lambda i: (i, 0)),
        ],
        out_shape=[
            jax.ShapeDtypeStruct((n, PA_WIDTH), BF16),
            jax.ShapeDtypeStruct((n, PG_WIDTH), GATE_DTYPE),
        ],
        compiler_params=_cparams("arbitrary"),
        name="in_proj",
    )(x, mod, nw, w)


def _tri_masks():
    row = lax.broadcasted_iota(jnp.int32, (CHUNK, CHUNK), 0)
    col = lax.broadcasted_iota(jnp.int32, (CHUNK, CHUNK), 1)
    return col <= row, col >= row


def _chunk_decays(logg, mask, fwd):
    tri = jnp.where(mask, 1.0, 0.0).astype(BF16)
    hi, lo = _split_bf16(logg)
    cs = _dot(tri, jnp.concatenate([hi, lo], axis=1))
    w = logg.shape[1]
    b = cs[:, :w] + cs[:, w:]
    r = CHUNK // 2 if fwd else CHUNK // 2 - 1
    last = CHUNK - 1 if fwd else 0
    b_ref = b[r:r + 1, :]
    b_last = b[last:last + 1, :]
    return (jnp.exp(b - b_ref), jnp.exp(b_ref - b), jnp.exp(b_ref), jnp.exp(b_last - b_ref),
            jnp.exp(b_last))


def _head_chunk(qt, kt, qb, kd, v, s_t, mask):
    p = jnp.where(mask, _dot_nt(qt, kt), 0.0).astype(BF16)
    o = _dot(p, v) + _dot_nt(qb, s_t.astype(BF16))
    return o, _dot_tn(v, kd)


def _hg_gate(z, lb):
    e = jnp.exp(-jnp.abs(z))
    r = 1.0 / (1.0 + e)
    er = e * r
    pos = z >= 0.0
    sig = jnp.where(pos, r, er)
    sig_neg = jnp.where(pos, er, r)
    oml = 1.0 - lb
    return oml * sig_neg, jnp.log(lb + oml * sig)


def _hg_kernel(q_ref, i_ref, g_ref, zf_ref, zb_ref, lb_ref, nw_ref, o_ref, of_ref, s_ref):
    n_chunks = q_ref.shape[0] // CHUNK
    mask_f, mask_b = _tri_masks()
    lb = lb_ref[...]
    nw = nw_ref[...]

    def direction(z_ref, lbd, mask, fwd, c):
        rows = pl.ds(pl.multiple_of(c * CHUNK, CHUNK), CHUNK)
        q = q_ref[rows, :].astype(F32)
        q = q * _sigmoid(q) * (HG_DK ** -0.5)
        k, logf = _hg_gate(z_ref[rows, :].astype(F32), lbd)
        a, ainv, e_ref, e_last_ref, decay = _chunk_decays(logf, mask, fwd)
        qa = q * a
        ka = k * ainv
        s_t = s_ref[...]
        o, ds_t = _head_chunk(qa.astype(BF16), ka.astype(BF16), (qa * e_ref).astype(BF16),
                              (ka * e_last_ref).astype(BF16), i_ref[rows, :], s_t, mask)
        s_ref[...] = decay * s_t + ds_t
        return rows, o

    s_ref[...] = jnp.zeros_like(s_ref)

    def fwd_body(c, carry):
        rows, o = direction(zf_ref, lb[0:1, :], mask_f, True, c)
        of_ref[rows, :] = o
        return carry

    lax.fori_loop(0, n_chunks, fwd_body, 0)
    s_ref[...] = jnp.zeros_like(s_ref)

    def bwd_body(j, carry):
        rows, o = direction(zb_ref, lb[1:2, :], mask_b, False, n_chunks - 1 - j)
        o = _rms(o + of_ref[rows, :], nw)
        g = g_ref[rows, :].astype(F32)
        o_ref[rows, :] = (o * (g * _sigmoid(g))).astype(o_ref.dtype)
        return carry

    lax.fori_loop(0, n_chunks, bwd_body, 0)


def _hg_mixer(pa, pg, lb, nw, seq_len):
    n = pa.shape[0]
    nh = HG_HEADS
    blk = lambda off: pl.BlockSpec((seq_len, HG_DK), lambda b, h: (b, off + h))
    return pl.pallas_call(
        _hg_kernel,
        grid=(n // seq_len, nh),
        in_specs=[
            blk(0), blk(nh), blk(2 * nh),
            blk(0), blk(nh),
            pl.BlockSpec((2, HG_DK), lambda b, h: (0, h)),
            pl.BlockSpec((1, HG_DK), lambda b, h: (0, 0)),
        ],
        out_specs=pl.BlockSpec((seq_len, HG_DK), lambda b, h: (b, h)),
        out_shape=jax.ShapeDtypeStruct((n, HG_WIDTH), BF16),
        scratch_shapes=[pltpu.VMEM((seq_len, HG_DK), F32), pltpu.VMEM((HG_DK, HG_DK), F32)],
        compiler_params=_cparams("arbitrary", "arbitrary"),
        name="hg_mixer",
    )(pa, pa, pa, pg, pg, lb, nw)


def _gla_kernel(q_ref, k_ref, v_ref, g_ref, lr_ref, wup_ref, bup_ref, nw_ref, o_ref, of_ref, s_ref):
    n_chunks = q_ref.shape[0] // CHUNK
    mask_f, mask_b = _tri_masks()
    lane = lax.broadcasted_iota(jnp.int32, (1, LANES), 1)
    head_masks = (lane < GLA_DK, lane >= GLA_DK)
    nw = nw_ref[...]
    dv = GLA_DV

    def direction(d, mask, fwd, c):
        rows = pl.ds(pl.multiple_of(c * CHUNK, CHUNK), CHUNK)
        q = q_ref[rows, :].astype(F32) * (GLA_DK ** -0.5)
        k = k_ref[rows, :].astype(F32)
        z = _dot(lr_ref[rows, :].astype(BF16), wup_ref[:, d * LANES:(d + 1) * LANES])
        z = z + bup_ref[:, d * LANES:(d + 1) * LANES]
        logg = (jnp.minimum(z, 0.0) - jnp.log(1.0 + jnp.exp(-jnp.abs(z)))) * (1.0 / GLA_GATE_NORMALIZER)
        a, ainv, e_ref, e_last_ref, decay = _chunk_decays(logg, mask, fwd)
        qa = q * a
        ka = k * ainv
        qt = qa.astype(BF16)
        kt = ka.astype(BF16)
        qb = (qa * e_ref).astype(BF16)
        kd = (ka * e_last_ref).astype(BF16)
        zero = jnp.zeros_like(qt)
        outs = []
        for hh in range(2):
            hm = head_masks[hh]
            s_t = s_ref[hh]
            o, ds_t = _head_chunk(jnp.where(hm, qt, zero), kt, jnp.where(hm, qb, zero), kd,
                                  v_ref[rows, hh * dv:(hh + 1) * dv], s_t, mask)
            s_ref[hh] = decay * s_t + ds_t
            outs.append(o)
        return rows, outs

    s_ref[...] = jnp.zeros_like(s_ref)

    def fwd_body(c, carry):
        rows, outs = direction(0, mask_f, True, c)
        for hh in range(2):
            of_ref[rows, hh * dv:(hh + 1) * dv] = outs[hh]
        return carry

    lax.fori_loop(0, n_chunks, fwd_body, 0)
    s_ref[...] = jnp.zeros_like(s_ref)

    def bwd_body(j, carry):
        rows, outs = direction(1, mask_b, False, n_chunks - 1 - j)
        for hh in range(2):
            cols = slice(hh * dv, (hh + 1) * dv)
            o = _rms(outs[hh] + of_ref[rows, cols], nw)
            g = g_ref[rows, cols].astype(F32)
            o_ref[rows, cols] = (o * (g * _sigmoid(g))).astype(o_ref.dtype)
        return carry

    lax.fori_loop(0, n_chunks, bwd_body, 0)


def _gla_mixer(pa, pg, wup, bup, nw, seq_len):
    n = pa.shape[0]
    npair = GLA_HEADS // 2
    return pl.pallas_call(
        _gla_kernel,
        grid=(n // seq_len, npair),
        in_specs=[
            pl.BlockSpec((seq_len, LANES), lambda b, p: (b, 12 + p)),
            pl.BlockSpec((seq_len, LANES), lambda b, p: (b, 14 + p)),
            pl.BlockSpec((seq_len, 2 * GLA_DV), lambda b, p: (b, 8 + p)),
            pl.BlockSpec((seq_len, 2 * GLA_DV), lambda b, p: (b, 10 + p)),
            pl.BlockSpec((seq_len, LANES), lambda b, p: (b, 8)),
            pl.BlockSpec((LANES, 2 * LANES), lambda b, p: (0, p)),
            pl.BlockSpec((1, 2 * LANES), lambda b, p: (0, p)),
            pl.BlockSpec((1, GLA_DV), lambda b, p: (0, 0)),
        ],
        out_specs=pl.BlockSpec((seq_len, 2 * GLA_DV), lambda b, p: (b, p)),
        out_shape=jax.ShapeDtypeStruct((n, GLA_WIDTH), BF16),
        scratch_shapes=[pltpu.VMEM((seq_len, 2 * GLA_DV), F32), pltpu.VMEM((2, GLA_DV, LANES), F32)],
        compiler_params=_cparams("arbitrary", "arbitrary"),
        name="gla_mixer",
    )(pa, pa, pa, pa, pg, wup, bup, nw)


def _out_proj_body(x_ref, mh_ref, mg_ref, mod_ref, w_ref, nw_ref):
    m = mod_ref[0]
    y = _dot(mh_ref[...], w_ref[0:HG_WIDTH, :]) + _dot(mg_ref[...], w_ref[HG_WIDTH:, :])
    x = x_ref[...] + m[2:3, :] * y
    h = _rms(x, nw_ref[...]) * (1.0 + m[4:5, :]) + m[3:4, :]
    return x, h


def _out_proj_kernel(x_ref, mh_ref, mg_ref, mod_ref, w_ref, nw_ref, xo_ref, h_ref):
    x, h = _out_proj_body(x_ref, mh_ref, mg_ref, mod_ref, w_ref, nw_ref)
    xo_ref[...] = x
    h_ref[...] = h.astype(BF16)


def _out_proj_route_kernel(x_ref, mh_ref, mg_ref, mod_ref, w_ref, nw_ref, wr_ref, br_ref,
                           xo_ref, h_ref, route_ref):
    x, h = _out_proj_body(x_ref, mh_ref, mg_ref, mod_ref, w_ref, nw_ref)
    xo_ref[...] = x
    h_ref[...] = h.astype(BF16)
    wr_hi, wr_lo = _split_bf16(wr_ref[...])
    logits = _dot3(h, wr_hi, wr_lo) + br_ref[...]
    lane = lax.broadcasted_iota(jnp.int32, logits.shape, 1).astype(F32)
    neg = jnp.float32(-jnp.inf)
    logits = jnp.where(lane < N_EXPERTS, logits, neg)
    m1 = jnp.max(logits, axis=-1, keepdims=True)
    i1 = jnp.min(jnp.where(logits == m1, lane, float(LANES)), axis=-1, keepdims=True)
    rest = jnp.where(lane == i1, neg, logits)
    m2 = jnp.max(rest, axis=-1, keepdims=True)
    i2 = jnp.min(jnp.where(rest == m2, lane, float(LANES)), axis=-1, keepdims=True)
    e2 = jnp.exp(m2 - m1)
    den = 1.0 / (1.0 + e2)
    route = jnp.where(lane == 0.0, i1,
                      jnp.where(lane == 1.0, i2,
                                jnp.where(lane == 2.0, den, jnp.where(lane == 3.0, e2 * den, 0.0))))
    route_ref[...] = route


def _out_proj(x, mh, mg, mod, w, nw, seq_len, tm, router=None):
    n, d = x.shape
    tps = seq_len // tm
    in_specs = [
        pl.BlockSpec((tm, d), lambda i: (i, 0)),
        pl.BlockSpec((tm, HG_WIDTH), lambda i: (i, 0)),
        pl.BlockSpec((tm, GLA_WIDTH), lambda i: (i, 0)),
        pl.BlockSpec((1, 6, d), lambda i: (i // tps, 0, 0)),
        pl.BlockSpec((d, d), lambda i: (0, 0)),
        pl.BlockSpec((1, d), lambda i: (0, 0)),
    ]
    out_specs = [pl.BlockSpec((tm, d), lambda i: (i, 0)), pl.BlockSpec((tm, d), lambda i: (i, 0))]
    out_shape = [jax.ShapeDtypeStruct((n, d), F32), jax.ShapeDtypeStruct((n, d), BF16)]
    args = [x, mh, mg, mod, w, nw]
    body = _out_proj_kernel
    if router is not None:
        in_specs += [pl.BlockSpec((d, LANES), lambda i: (0, 0)), pl.BlockSpec((1, LANES), lambda i: (0, 0))]
        out_specs.append(pl.BlockSpec((tm, LANES), lambda i: (i, 0)))
        out_shape.append(jax.ShapeDtypeStruct((n, LANES), F32))
        args += list(router)
        body = _out_proj_route_kernel
    return pl.pallas_call(
        body,
        grid=(n // tm,),
        in_specs=in_specs,
        out_specs=out_specs,
        out_shape=out_shape,
        compiler_params=_cparams("arbitrary"),
        name="out_proj",
    )(*args)


def _swiglu_step(x_ref, w1_ref, w3_ref, w2_ref, acc_ref):
    f = pl.program_id(1)
    xb = x_ref[...]
    a = _dot(xb, w1_ref[0])
    b = _dot(xb, w3_ref[0])
    part = _dot((a * _sigmoid(a) * b).astype(BF16), w2_ref[0])

    @pl.when(f == 0)
    def _():
        acc_ref[...] = part

    @pl.when(f > 0)
    def _():
        acc_ref[...] += part


def _dense_ffn_kernel(h_ref, x_ref, mod_ref, w1_ref, w3_ref, w2_ref, o_ref, acc_ref):
    _swiglu_step(h_ref, w1_ref, w3_ref, w2_ref, acc_ref)

    @pl.when(pl.program_id(1) == pl.num_programs(1) - 1)
    def _():
        o_ref[...] = x_ref[...] + mod_ref[0][5:6, :] * acc_ref[...]


def _dense_ffn(h, x, mod, w1, w3, w2, seq_len, tm, tf):
    n, d = x.shape
    ff = w1.shape[-1]
    tps = seq_len // tm
    return pl.pallas_call(
        _dense_ffn_kernel,
        grid=(n // tm, ff // tf),
        in_specs=[
            pl.BlockSpec((tm, d), lambda i, f: (i, 0)),
            pl.BlockSpec((tm, d), lambda i, f: (i, 0)),
            pl.BlockSpec((1, 6, d), lambda i, f: (i // tps, 0, 0)),
            pl.BlockSpec((1, d, tf), lambda i, f: (0, 0, f)),
            pl.BlockSpec((1, d, tf), lambda i, f: (0, 0, f)),
            pl.BlockSpec((1, tf, d), lambda i, f: (0, f, 0)),
        ],
        out_specs=pl.BlockSpec((tm, d), lambda i, f: (i, 0)),
        out_shape=jax.ShapeDtypeStruct((n, d), F32),
        scratch_shapes=[pltpu.VMEM((tm, d), F32)],
        compiler_params=_cparams("arbitrary", "arbitrary"),
        name="dense_ffn",
    )(h, x, mod, w1, w3, w2)


def _expert_ffn_kernel(be_ref, nu_ref, xs_ref, w1_ref, w3_ref, w2_ref, o_ref, acc_ref):
    i = pl.program_id(0)
    last = pl.program_id(1) == pl.num_programs(1) - 1
    used = i < nu_ref[0]

    @pl.when(used)
    def _():
        _swiglu_step(xs_ref, w1_ref, w3_ref, w2_ref, acc_ref)

    @pl.when(jnp.logical_and(used, last))
    def _():
        o_ref[...] = acc_ref[...].astype(o_ref.dtype)

    @pl.when(jnp.logical_and(jnp.logical_not(used), last))
    def _():
        o_ref[...] = jnp.zeros_like(o_ref)


def _expert_ffn(xs, block_e, n_used, w1, w3, w2, tm, tf):
    rows, d = xs.shape
    ff = w1.shape[-1]
    nf = ff // tf

    def row_idx(i, f, be, nu):
        return (jnp.minimum(i, nu[0] - 1), 0)

    def fsel(i, f, nu):
        return jnp.where(i < nu[0], f, nf - 1)

    grid_spec = pltpu.PrefetchScalarGridSpec(
        num_scalar_prefetch=2,
        grid=(rows // tm, nf),
        in_specs=[
            pl.BlockSpec((tm, d), row_idx),
            pl.BlockSpec((1, d, tf), lambda i, f, be, nu: (be[i], 0, fsel(i, f, nu))),
            pl.BlockSpec((1, d, tf), lambda i, f, be, nu: (be[i], 0, fsel(i, f, nu))),
            pl.BlockSpec((1, tf, d), lambda i, f, be, nu: (be[i], fsel(i, f, nu), 0)),
        ],
        out_specs=pl.BlockSpec((tm, d), lambda i, f, be, nu: (i, 0)),
        scratch_shapes=[pltpu.VMEM((tm, d), F32)],
    )
    return pl.pallas_call(
        _expert_ffn_kernel,
        grid_spec=grid_spec,
        out_shape=jax.ShapeDtypeStruct((rows, d), BF16),
        compiler_params=_cparams("arbitrary", "arbitrary"),
        name="expert_ffn",
    )(block_e, n_used, xs, w1, w3, w2)


def _final_norm_kernel(x_ref, w_ref, o_ref):
    o_ref[...] = _rms(x_ref[...], w_ref[...])


def _final_norm(x, w, tm):
    n, d = x.shape
    return pl.pallas_call(
        _final_norm_kernel,
        grid=(n // tm,),
        in_specs=[pl.BlockSpec((tm, d), lambda i: (i, 0)), pl.BlockSpec((1, d), lambda i: (0, 0))],
        out_specs=pl.BlockSpec((tm, d), lambda i: (i, 0)),
        out_shape=jax.ShapeDtypeStruct((n, d), F32),
        compiler_params=_cparams("arbitrary"),
        name="final_norm",
    )(x, w)


def _moe(h, x, g2, route, w1, w3, w2, tm, tf):
    n, d = x.shape
    n_asg = n * TOP_K
    flat_e = route[:, :TOP_K].astype(jnp.int32).reshape(-1)
    gates = route[:, TOP_K:2 * TOP_K]
    onehot = (flat_e[:, None] == jnp.arange(N_EXPERTS, dtype=jnp.int32)[None, :]).astype(jnp.int32)
    csum = jnp.cumsum(onehot, axis=0)
    counts = csum[-1]
    rank = jnp.take_along_axis(csum, flat_e[:, None], axis=1)[:, 0] - 1
    pcounts = (counts + tm - 1) // tm * tm
    pend = jnp.cumsum(pcounts)
    pstart = pend - pcounts
    dest = pstart[flat_e] + rank
    n_blocks = -(-n_asg // tm) + N_EXPERTS
    tok = jnp.arange(n_asg, dtype=jnp.int32) // TOP_K
    src_tok = jnp.zeros((n_blocks * tm,), jnp.int32).at[dest].set(tok)
    block_e = jnp.minimum(
        jnp.searchsorted(pend, jnp.arange(n_blocks, dtype=pend.dtype) * tm, side='right'),
        N_EXPERTS - 1).astype(jnp.int32)
    n_used = (pend[-1:] // tm).astype(jnp.int32)
    block_e = jnp.where(jnp.arange(n_blocks) < n_used[0], block_e, block_e[jnp.maximum(n_used[0] - 1, 0)])
    xs = jnp.take(h, src_tok, axis=0)
    ys = _expert_ffn(xs, block_e, n_used, w1, w3, w2, tm, tf)
    dst = dest.reshape(n, TOP_K)
    f = (gates[:, 0:1] * jnp.take(ys, dst[:, 0], axis=0).astype(F32)
         + gates[:, 1:2] * jnp.take(ys, dst[:, 1], axis=0).astype(F32))
    return x + g2 * f


def _prep_weights(w_in, gla_w_gk_up, gla_b_gk, hg_lb_logits):
    depth = w_in.shape[0]
    sizes = (HG_WIDTH,) * 5 + (GLA_QK, GLA_QK, GLA_WIDTH, GLA_WIDTH, GLA_GATE_RANK, GLA_GATE_RANK)
    offs = [0]
    for s in sizes:
        offs.append(offs[-1] + s)
    seg = lambda k: w_in[:, :, offs[k]:offs[k + 1]]
    hq, hf_f, hf_b, hi, hg, gq, gk, gv, gg, lr_f, lr_b = [seg(k) for k in range(11)]
    pad = jnp.zeros(w_in.shape[:2] + (LANES - 2 * GLA_GATE_RANK,), w_in.dtype)
    w_perm = jnp.concatenate([hq, hi, hg, gq, gk, gv, gg, hf_f, hf_b, lr_f, lr_b, pad], axis=-1).astype(BF16)

    r = GLA_GATE_RANK
    wup = jnp.zeros((depth, LANES, GLA_HEADS // 2, 2, LANES), F32)
    bup = jnp.zeros((depth, GLA_HEADS // 2, 2, LANES), F32)
    for dd in range(2):
        wup = wup.at[:, dd * r:(dd + 1) * r, :, dd, :].set(
            gla_w_gk_up[:, dd].reshape(depth, r, GLA_HEADS // 2, LANES))
        bup = bup.at[:, :, dd, :].set(gla_b_gk[:, dd].reshape(depth, GLA_HEADS // 2, LANES))
    wup = wup.reshape(depth, LANES, GLA_HEADS * LANES).astype(BF16)
    bup = bup.reshape(depth, 1, GLA_HEADS * LANES)

    lb = jnp.cumsum(jax.nn.softmax(hg_lb_logits.astype(F32), axis=0), axis=0)
    lb = lb - lb[0]
    return w_perm, wup, bup, lb


def _pick(pref, total):
    t = min(pref, total)
    while total % t:
        t //= 2
    return t


def _trunk(x3, mods, weights):
    (norm1_w, w_perm, lb, wup, bup, hg_norm_w, gla_norm_w, w_out, norm2_w, w_ff1, w_ff3, w_ff2,
     w_router, b_router, w_e1, w_e3, w_e2, final_norm_w) = weights
    nb, seq_len, d = x3.shape
    depth = norm1_w.shape[0]
    n = nb * seq_len
    x = x3.reshape(n, d)
    tm = _pick(512, seq_len)
    tm_ffn = _pick(1024, seq_len)
    tm_moe = _pick(1024, n * TOP_K)
    for l in range(depth):
        mod = mods[l]
        pa, pg = _in_proj(x, mod, norm1_w[l][None, :], w_perm[l], seq_len, tm)
        mh = _hg_mixer(pa, pg, lb[l], hg_norm_w[l][None, :], seq_len)
        mg = _gla_mixer(pa, pg, wup[l], bup[l], gla_norm_w[l][None, :], seq_len)
        m = l // 2
        if l % 2 == 0:
            x, h = _out_proj(x, mh, mg, mod, w_out[l], norm2_w[l][None, :], seq_len, tm)
            x = _dense_ffn(h, x, mod, w_ff1[m:m + 1], w_ff3[m:m + 1], w_ff2[m:m + 1], seq_len, tm_ffn,
                           _pick(256, w_ff1.shape[-1]))
        else:
            wr = jnp.pad(w_router[m], ((0, 0), (0, LANES - N_EXPERTS)))
            br = jnp.pad(b_router[m], (0, LANES - N_EXPERTS))[None, :]
            x, h, route = _out_proj(x, mh, mg, mod, w_out[l], norm2_w[l][None, :], seq_len, tm,
                                    router=(wr, br))
            g2 = jnp.repeat(mod[:, 5, :], seq_len, axis=0)
            x = _moe(h, x, g2, route, w_e1[m], w_e3[m], w_e2[m], tm_moe, _pick(512, w_e1.shape[-1]))
    return _final_norm(x, final_norm_w[None, :], tm).reshape(nb, seq_len, d)


def kernel(x_prompt, x_sample, c_prompt, c_sample, w_ada, b_ada, norm1_w, w_in, hg_lb_logits, gla_w_gk_up, gla_b_gk, hg_norm_w, gla_norm_w, w_out, norm2_w, w_ff1, w_ff3, w_ff2, w_router, b_router, w_e1, w_e3, w_e2, final_norm_w):
    depth, d = norm1_w.shape
    w_perm, wup, bup, lb = _prep_weights(w_in, gla_w_gk_up, gla_b_gk, hg_lb_logits)
    nbp = c_prompt.shape[0]
    mods = _ada_mods(jnp.concatenate([c_prompt, c_sample], axis=0), w_ada, b_ada)
    mods = mods.reshape(depth, mods.shape[1], 6, d)
    weights = (norm1_w, w_perm, lb, wup, bup, hg_norm_w, gla_norm_w, w_out.astype(BF16), norm2_w,
               w_ff1.astype(BF16), w_ff3.astype(BF16), w_ff2.astype(BF16), w_router, b_router,
               w_e1.astype(BF16), w_e3.astype(BF16), w_e2.astype(BF16), final_norm_w)
    y_prompt = _trunk(x_prompt, mods[:, :nbp], weights)
    y_sample = _trunk(x_sample, mods[:, nbp:], weights)
    return (y_prompt, y_sample)
```

```python
import functools

import jax
import jax.numpy as jnp
from jax import lax
from jax.experimental import pallas as pl
from jax.experimental.pallas import tpu as pltpu

F32 = jnp.float32
BF16 = jnp.bfloat16

D_MODEL = 1024
HG_HEADS = 4
HG_DK = 128
HG_WIDTH = 512
GLA_HEADS = 4
GLA_DK = 64
GLA_DV = 128
GLA_QK = 256
GLA_WIDTH = 512
GLA_GATE_RANK = 16
GLA_GATE_NORMALIZER = 16.0
CHUNK = 64
N_EXPERTS = 8
TOP_K = 2
EPS = 1e-6
LANES = 128

PA_WIDTH = 3 * HG_WIDTH + 2 * GLA_QK + 2 * GLA_WIDTH
PG_WIDTH = 2 * HG_WIDTH + LANES
GATE_DTYPE = F32

VMEM_LIMIT = 48 * 1024 * 1024
MIXER_BLOCK_ROWS = 256


def _cparams(*sem):
    return pltpu.CompilerParams(dimension_semantics=sem, vmem_limit_bytes=VMEM_LIMIT)


def _sigmoid(x):
    return 1.0 / (1.0 + jnp.exp(-x))


def _split_bf16(a):
    hi = a.astype(BF16)
    lo = (a - hi.astype(F32)).astype(BF16)
    return hi, lo


def _dot(a, b):
    return jnp.dot(a, b, preferred_element_type=F32)


def _dot_nt(a, b):
    return lax.dot_general(a, b, (((1,), (1,)), ((), ())), preferred_element_type=F32)


def _dot_tn(a, b):
    return lax.dot_general(a, b, (((0,), (0,)), ((), ())), preferred_element_type=F32)


def _dot3(a, b_hi, b_lo):
    a_hi, a_lo = _split_bf16(a)
    return _dot(a_hi, b_hi) + (_dot(a_hi, b_lo) + _dot(a_lo, b_hi))


def _rms(x, w):
    ms = jnp.mean(x * x, axis=-1, keepdims=True)
    return x * lax.rsqrt(ms + EPS) * w


def _ada_kernel(c_ref, w_ref, b_ref, o_ref):
    c = c_ref[...]
    s = c * _sigmoid(c)
    w_hi, w_lo = _split_bf16(w_ref[0])
    o_ref[0] = _dot3(s, w_hi, w_lo) + b_ref[0]


def _ada_mods(c, w_ada, b_ada):
    depth, d, e = w_ada.shape
    nb = c.shape[0]
    tn = 1536
    return pl.pallas_call(
        _ada_kernel,
        grid=(depth, e // tn),
        in_specs=[
            pl.BlockSpec((nb, d), lambda l, j: (0, 0)),
            pl.BlockSpec((1, d, tn), lambda l, j: (l, 0, j)),
            pl.BlockSpec((1, 1, tn), lambda l, j: (l, 0, j)),
        ],
        out_specs=pl.BlockSpec((1, nb, tn), lambda l, j: (l, 0, j)),
        out_shape=jax.ShapeDtypeStruct((depth, nb, e), F32),
        compiler_params=_cparams("arbitrary", "arbitrary"),
        name="ada_mods",
    )(c, w_ada, b_ada.reshape(depth, 1, e))


def _in_proj_kernel(x_ref, mod_ref, nw_ref, w_ref, pa_ref, pg_ref):
    m = mod_ref[0]
    h = _rms(x_ref[...], nw_ref[...]) * (1.0 + m[1:2, :]) + m[0:1, :]
    hb = h.astype(BF16)
    cw = 512
    for c in range(PA_WIDTH // cw):
        pa_ref[:, c * cw:(c + 1) * cw] = _dot(hb, w_ref[:, c * cw:(c + 1) * cw]).astype(BF16)
    for lo, hi in ((0, 512), (512, 1024), (1024, PG_WIDTH)):
        pg_ref[:, lo:hi] = _dot(hb, w_ref[:, PA_WIDTH + lo:PA_WIDTH + hi]).astype(pg_ref.dtype)


def _in_proj(x, mod, nw, w, seq_len, tm):
    n, d = x.shape
    tps = seq_len // tm
    return pl.pallas_call(
        _in_proj_kernel,
        grid=(n // tm,),
        in_specs=[
            pl.BlockSpec((tm, d), lambda i: (i, 0)),
            pl.BlockSpec((1, 6, d), lambda i: (i // tps, 0, 0)),
            pl.BlockSpec((1, d), lambda i: (0, 0)),
            pl.BlockSpec((d, PA_WIDTH + PG_WIDTH), lambda i: (0, 0)),
        ],
        out_specs=[
            pl.BlockSpec((tm, PA_WIDTH), lambda i: (i, 0)),
            pl.BlockSpec((tm, PG_WIDTH), lambda i: (i, 0)),
        ],
        out_shape=[
            jax.ShapeDtypeStruct((n, PA_WIDTH), BF16),
            jax.ShapeDtypeStruct((n, PG_WIDTH), GATE_DTYPE),
        ],
        compiler_params=_cparams("arbitrary"),
        name="in_proj",
    )(x, mod, nw, w)


def _block_masks(rows):
    r = jnp.arange(rows, dtype=jnp.int32)[:, None]
    c = jnp.arange(rows, dtype=jnp.int32)[None, :]
    same = (r // CHUNK) == (c // CHUNK)
    return jnp.stack([same & (c <= r), same & (c >= r)]).astype(F32)


def _chunk_slices(a, n_sub):
    return [a[c * CHUNK:(c + 1) * CHUNK] for c in range(n_sub)]


def _decay_factors(logg, tri, fwd):
    n_sub = logg.shape[0] // CHUNK
    w = logg.shape[1]
    hi, lo = _split_bf16(logg)
    cs = _dot(tri, jnp.concatenate([hi, lo], axis=1))
    b = cs[:, :w] + cs[:, w:]
    r = CHUNK // 2 if fwd else CHUNK // 2 - 1
    last = CHUNK - 1 if fwd else 0
    out = []
    for bc in _chunk_slices(b, n_sub):
        b_ref = bc[r:r + 1, :]
        b_last = bc[last:last + 1, :]
        out.append((jnp.exp(bc - b_ref), jnp.exp(b_ref - bc), jnp.exp(b_ref), jnp.exp(b_last - b_ref),
                    jnp.exp(b_last)))
    return out


def _block_recurrence(q, k, logg, v_heads, q_masks, maskf, s_ref, d, fwd):
    n_sub = q.shape[0] // CHUNK
    fac = _decay_factors(logg, maskf.astype(BF16), fwd)
    qs = _chunk_slices(q, n_sub)
    ks = _chunk_slices(k, n_sub)
    qa = [qs[c] * fac[c][0] for c in range(n_sub)]
    ka = [ks[c] * fac[c][1] for c in range(n_sub)]
    qt = jnp.concatenate(qa, axis=0).astype(BF16)
    kt = jnp.concatenate(ka, axis=0).astype(BF16)
    qb = [(qa[c] * fac[c][2]).astype(BF16) for c in range(n_sub)]
    kd = [(ka[c] * fac[c][3]).astype(BF16) for c in range(n_sub)]
    mask = maskf > 0.5
    order = range(n_sub) if fwd else range(n_sub - 1, -1, -1)
    outs = []
    for h, v in enumerate(v_heads):
        hm = q_masks[h]
        sel = (lambda a: a) if hm is None else (lambda a, hm=hm: jnp.where(hm, a, jnp.zeros_like(a)))
        p = jnp.where(mask, _dot_nt(sel(qt), kt), 0.0).astype(BF16)
        o_intra = _chunk_slices(_dot(p, v), n_sub)
        vs = _chunk_slices(v, n_sub)
        s_t = s_ref[d, h]
        o = [None] * n_sub
        for c in order:
            o[c] = o_intra[c] + _dot_nt(sel(qb[c]), s_t.astype(BF16))
            s_t = fac[c][4] * s_t + _dot_tn(vs[c], kd[c])
        s_ref[d, h] = s_t
        outs.append(jnp.concatenate(o, axis=0))
    return outs


def _two_stage(n_blocks, block_fn, acc_ref, finalize):
    half = n_blocks // 2

    def stage1(j, carry):
        for d in range(2):
            rows, outs = block_fn(d, j if d == 0 else n_blocks - 1 - j)
            for h, o in enumerate(outs):
                acc_ref[rows, h * o.shape[1]:(h + 1) * o.shape[1]] = o
        return carry

    def stage2(j, carry):
        for d in range(2):
            rows, outs = block_fn(d, j if d == 0 else n_blocks - 1 - j)
            for h, o in enumerate(outs):
                cols = slice(h * o.shape[1], (h + 1) * o.shape[1])
                finalize(rows, cols, o + acc_ref[rows, cols])
        return carry

    lax.fori_loop(0, half, stage1, 0)
    lax.fori_loop(half, n_blocks, stage2, 0)


def _hg_gate(z, lb):
    e = jnp.exp(-jnp.abs(z))
    r = 1.0 / (1.0 + e)
    er = e * r
    pos = z >= 0.0
    sig = jnp.where(pos, r, er)
    sig_neg = jnp.where(pos, er, r)
    oml = 1.0 - lb
    return oml * sig_neg, jnp.log(lb + oml * sig)


def _hg_kernel(q_ref, i_ref, g_ref, zf_ref, zb_ref, lb_ref, nw_ref, mask_ref, o_ref, acc_ref, s_ref):
    blk_rows = mask_ref.shape[1]
    n_blocks = q_ref.shape[0] // blk_rows
    lb = lb_ref[...]
    nw = nw_ref[...]
    z_refs = (zf_ref, zb_ref)

    def block_fn(d, blk):
        rows = pl.ds(pl.multiple_of(blk * blk_rows, blk_rows), blk_rows)
        q = q_ref[rows, :].astype(F32)
        q = q * _sigmoid(q) * (HG_DK ** -0.5)
        k, logf = _hg_gate(z_refs[d][rows, :].astype(F32), lb[d:d + 1, :])
        return rows, _block_recurrence(q, k, logf, [i_ref[rows, :]], [None], mask_ref[d], s_ref, d, d == 0)

    def finalize(rows, cols, o):
        g = g_ref[rows, cols].astype(F32)
        o_ref[rows, cols] = (_rms(o, nw) * (g * _sigmoid(g))).astype(o_ref.dtype)

    s_ref[...] = jnp.zeros_like(s_ref)
    _two_stage(n_blocks, block_fn, acc_ref, finalize)


def _mixer_block_rows(seq_len):
    return min(MIXER_BLOCK_ROWS, seq_len // 2)


def _hg_mixer(pa, pg, lb, nw, seq_len):
    n = pa.shape[0]
    nh = HG_HEADS
    br = _mixer_block_rows(seq_len)
    blk = lambda off: pl.BlockSpec((seq_len, HG_DK), lambda b, h: (b, off + h))
    return pl.pallas_call(
        _hg_kernel,
        grid=(n // seq_len, nh),
        in_specs=[
            blk(0), blk(nh), blk(2 * nh),
            blk(0), blk(nh),
            pl.BlockSpec((2, HG_DK), lambda b, h: (0, h)),
            pl.BlockSpec((1, HG_DK), lambda b, h: (0, 0)),
            pl.BlockSpec((2, br, br), lambda b, h: (0, 0, 0)),
        ],
        out_specs=pl.BlockSpec((seq_len, HG_DK), lambda b, h: (b, h)),
        out_shape=jax.ShapeDtypeStruct((n, HG_WIDTH), BF16),
        scratch_shapes=[pltpu.VMEM((seq_len, HG_DK), F32), pltpu.VMEM((2, 1, HG_DK, HG_DK), F32)],
        compiler_params=_cparams("arbitrary", "arbitrary"),
        name="hg_mixer",
    )(pa, pa, pa, pg, pg, lb, nw, _block_masks(br))


def _gla_kernel(q_ref, k_ref, v_ref, g_ref, lr_ref, wup_ref, bup_ref, nw_ref, mask_ref, o_ref, acc_ref, s_ref):
    blk_rows = mask_ref.shape[1]
    n_blocks = q_ref.shape[0] // blk_rows
    lane = lax.broadcasted_iota(jnp.int32, (1, LANES), 1)
    head_masks = [lane < GLA_DK, lane >= GLA_DK]
    nw = nw_ref[...]
    dv = GLA_DV

    def block_fn(d, blk):
        rows = pl.ds(pl.multiple_of(blk * blk_rows, blk_rows), blk_rows)
        q = q_ref[rows, :].astype(F32) * (GLA_DK ** -0.5)
        k = k_ref[rows, :].astype(F32)
        z = _dot(lr_ref[rows, :].astype(BF16), wup_ref[:, d * LANES:(d + 1) * LANES])
        z = z + bup_ref[:, d * LANES:(d + 1) * LANES]
        logg = (jnp.minimum(z, 0.0) - jnp.log(1.0 + jnp.exp(-jnp.abs(z)))) * (1.0 / GLA_GATE_NORMALIZER)
        v_heads = [v_ref[rows, hh * dv:(hh + 1) * dv] for hh in range(2)]
        return rows, _block_recurrence(q, k, logg, v_heads, head_masks, mask_ref[d], s_ref, d, d == 0)

    def finalize(rows, cols, o):
        g = g_ref[rows, cols].astype(F32)
        o_ref[rows, cols] = (_rms(o, nw) * (g * _sigmoid(g))).astype(o_ref.dtype)

    s_ref[...] = jnp.zeros_like(s_ref)
    _two_stage(n_blocks, block_fn, acc_ref, finalize)


def _gla_mixer(pa, pg, wup, bup, nw, seq_len):
    n = pa.shape[0]
    npair = GLA_HEADS // 2
    br = _mixer_block_rows(seq_len)
    return pl.pallas_call(
        _gla_kernel,
        grid=(n // seq_len, npair),
        in_specs=[
            pl.BlockSpec((seq_len, LANES), lambda b, p: (b, 12 + p)),
            pl.BlockSpec((seq_len, LANES), lambda b, p: (b, 14 + p)),
            pl.BlockSpec((seq_len, 2 * GLA_DV), lambda b, p: (b, 8 + p)),
            pl.BlockSpec((seq_len, 2 * GLA_DV), lambda b, p: (b, 10 + p)),
            pl.BlockSpec((seq_len, LANES), lambda b, p: (b, 8)),
            pl.BlockSpec((LANES, 2 * LANES), lambda b, p: (0, p)),
            pl.BlockSpec((1, 2 * LANES), lambda b, p: (0, p)),
            pl.BlockSpec((1, GLA_DV), lambda b, p: (0, 0)),
            pl.BlockSpec((2, br, br), lambda b, p: (0, 0, 0)),
        ],
        out_specs=pl.BlockSpec((seq_len, 2 * GLA_DV), lambda b, p: (b, p)),
        out_shape=jax.ShapeDtypeStruct((n, GLA_WIDTH), BF16),
        scratch_shapes=[pltpu.VMEM((seq_len, 2 * GLA_DV), F32), pltpu.VMEM((2, 2, GLA_DV, LANES), F32)],
        compiler_params=_cparams("arbitrary", "arbitrary"),
        name="gla_mixer",
    )(pa, pa, pa, pa, pg, wup, bup, nw, _block_masks(br))


def _out_proj_body(x_ref, mh_ref, mg_ref, mod_ref, w_ref, nw_ref):
    m = mod_ref[0]
    y = _dot(mh_ref[...], w_ref[0:HG_WIDTH, :]) + _dot(mg_ref[...], w_ref[HG_WIDTH:, :])
    x = x_ref[...] + m[2:3, :] * y
    h = _rms(x, nw_ref[...]) * (1.0 + m[4:5, :]) + m[3:4, :]
    return x, h


def _out_proj_kernel(x_ref, mh_ref, mg_ref, mod_ref, w_ref, nw_ref, xo_ref, h_ref):
    x, h = _out_proj_body(x_ref, mh_ref, mg_ref, mod_ref, w_ref, nw_ref)
    xo_ref[...] = x
    h_ref[...] = h.astype(BF16)


def _out_proj_route_kernel(x_ref, mh_ref, mg_ref, mod_ref, w_ref, nw_ref, wr_ref, br_ref,
                           xo_ref, h_ref, route_ref):
    x, h = _out_proj_body(x_ref, mh_ref, mg_ref, mod_ref, w_ref, nw_ref)
    xo_ref[...] = x
    h_ref[...] = h.astype(BF16)
    wr_hi, wr_lo = _split_bf16(wr_ref[...])
    logits = _dot3(h, wr_hi, wr_lo) + br_ref[...]
    lane = lax.broadcasted_iota(jnp.int32, logits.shape, 1).astype(F32)
    neg = jnp.float32(-jnp.inf)
    logits = jnp.where(lane < N_EXPERTS, logits, neg)
    m1 = jnp.max(logits, axis=-1, keepdims=True)
    i1 = jnp.min(jnp.where(logits == m1, lane, float(LANES)), axis=-1, keepdims=True)
    rest = jnp.where(lane == i1, neg, logits)
    m2 = jnp.max(rest, axis=-1, keepdims=True)
    i2 = jnp.min(jnp.where(rest == m2, lane, float(LANES)), axis=-1, keepdims=True)
    e2 = jnp.exp(m2 - m1)
    den = 1.0 / (1.0 + e2)
    route = jnp.where(lane == 0.0, i1,
                      jnp.where(lane == 1.0, i2,
                                jnp.where(lane == 2.0, den, jnp.where(lane == 3.0, e2 * den, 0.0))))
    route_ref[...] = route


def _out_proj(x, mh, mg, mod, w, nw, seq_len, tm, router=None):
    n, d = x.shape
    tps = seq_len // tm
    in_specs = [
        pl.BlockSpec((tm, d), lambda i: (i, 0)),
        pl.BlockSpec((tm, HG_WIDTH), lambda i: (i, 0)),
        pl.BlockSpec((tm, GLA_WIDTH), lambda i: (i, 0)),
        pl.BlockSpec((1, 6, d), lambda i: (i // tps, 0, 0)),
        pl.BlockSpec((d, d), lambda i: (0, 0)),
        pl.BlockSpec((1, d), lambda i: (0, 0)),
    ]
    out_specs = [pl.BlockSpec((tm, d), lambda i: (i, 0)), pl.BlockSpec((tm, d), lambda i: (i, 0))]
    out_shape = [jax.ShapeDtypeStruct((n, d), F32), jax.ShapeDtypeStruct((n, d), BF16)]
    args = [x, mh, mg, mod, w, nw]
    body = _out_proj_kernel
    if router is not None:
        in_specs += [pl.BlockSpec((d, LANES), lambda i: (0, 0)), pl.BlockSpec((1, LANES), lambda i: (0, 0))]
        out_specs.append(pl.BlockSpec((tm, LANES), lambda i: (i, 0)))
        out_shape.append(jax.ShapeDtypeStruct((n, LANES), F32))
        args += list(router)
        body = _out_proj_route_kernel
    return pl.pallas_call(
        body,
        grid=(n // tm,),
        in_specs=in_specs,
        out_specs=out_specs,
        out_shape=out_shape,
        compiler_params=_cparams("arbitrary"),
        name="out_proj",
    )(*args)


def _swiglu_step(x_ref, w1_ref, w3_ref, w2_ref, acc_ref):
    f = pl.program_id(1)
    xb = x_ref[...]
    a = _dot(xb, w1_ref[0])
    b = _dot(xb, w3_ref[0])
    part = _dot((a * _sigmoid(a) * b).astype(BF16), w2_ref[0])

    @pl.when(f == 0)
    def _():
        acc_ref[...] = part

    @pl.when(f > 0)
    def _():
        acc_ref[...] += part


def _dense_ffn_kernel(h_ref, x_ref, mod_ref, w1_ref, w3_ref, w2_ref, o_ref, acc_ref):
    _swiglu_step(h_ref, w1_ref, w3_ref, w2_ref, acc_ref)

    @pl.when(pl.program_id(1) == pl.num_programs(1) - 1)
    def _():
        o_ref[...] = x_ref[...] + mod_ref[0][5:6, :] * acc_ref[...]


def _dense_ffn(h, x, mod, w1, w3, w2, seq_len, tm, tf):
    n, d = x.shape
    ff = w1.shape[-1]
    tps = seq_len // tm
    return pl.pallas_call(
        _dense_ffn_kernel,
        grid=(n // tm, ff // tf),
        in_specs=[
            pl.BlockSpec((tm, d), lambda i, f: (i, 0)),
            pl.BlockSpec((tm, d), lambda i, f: (i, 0)),
            pl.BlockSpec((1, 6, d), lambda i, f: (i // tps, 0, 0)),
            pl.BlockSpec((1, d, tf), lambda i, f: (0, 0, f)),
            pl.BlockSpec((1, d, tf), lambda i, f: (0, 0, f)),
            pl.BlockSpec((1, tf, d), lambda i, f: (0, f, 0)),
        ],
        out_specs=pl.BlockSpec((tm, d), lambda i, f: (i, 0)),
        out_shape=jax.ShapeDtypeStruct((n, d), F32),
        scratch_shapes=[pltpu.VMEM((tm, d), F32)],
        compiler_params=_cparams("arbitrary", "arbitrary"),
        name="dense_ffn",
    )(h, x, mod, w1, w3, w2)


def _expert_ffn_kernel(be_ref, nu_ref, xs_ref, w1_ref, w3_ref, w2_ref, o_ref, acc_ref):
    i = pl.program_id(0)
    last = pl.program_id(1) == pl.num_programs(1) - 1
    used = i < nu_ref[0]

    @pl.when(used)
    def _():
        _swiglu_step(xs_ref, w1_ref, w3_ref, w2_ref, acc_ref)

    @pl.when(jnp.logical_and(used, last))
    def _():
        o_ref[...] = acc_ref[...].astype(o_ref.dtype)

    @pl.when(jnp.logical_and(jnp.logical_not(used), last))
    def _():
        o_ref[...] = jnp.zeros_like(o_ref)


def _expert_ffn(xs, block_e, n_used, w1, w3, w2, tm, tf):
    rows, d = xs.shape
    ff = w1.shape[-1]
    nf = ff // tf

    def row_idx(i, f, be, nu):
        return (jnp.minimum(i, nu[0] - 1), 0)

    def fsel(i, f, nu):
        return jnp.where(i < nu[0], f, nf - 1)

    grid_spec = pltpu.PrefetchScalarGridSpec(
        num_scalar_prefetch=2,
        grid=(rows // tm, nf),
        in_specs=[
            pl.BlockSpec((tm, d), row_idx),
            pl.BlockSpec((1, d, tf), lambda i, f, be, nu: (be[i], 0, fsel(i, f, nu))),
            pl.BlockSpec((1, d, tf), lambda i, f, be, nu: (be[i], 0, fsel(i, f, nu))),
            pl.BlockSpec((1, tf, d), lambda i, f, be, nu: (be[i], fsel(i, f, nu), 0)),
        ],
        out_specs=pl.BlockSpec((tm, d), lambda i, f, be, nu: (i, 0)),
        scratch_shapes=[pltpu.VMEM((tm, d), F32)],
    )
    return pl.pallas_call(
        _expert_ffn_kernel,
        grid_spec=grid_spec,
        out_shape=jax.ShapeDtypeStruct((rows, d), BF16),
        compiler_params=_cparams("arbitrary", "arbitrary"),
        name="expert_ffn",
    )(block_e, n_used, xs, w1, w3, w2)


def _final_norm_kernel(x_ref, w_ref, o_ref):
    o_ref[...] = _rms(x_ref[...], w_ref[...])


def _final_norm(x, w, tm):
    n, d = x.shape
    return pl.pallas_call(
        _final_norm_kernel,
        grid=(n // tm,),
        in_specs=[pl.BlockSpec((tm, d), lambda i: (i, 0)), pl.BlockSpec((1, d), lambda i: (0, 0))],
        out_specs=pl.BlockSpec((tm, d), lambda i: (i, 0)),
        out_shape=jax.ShapeDtypeStruct((n, d), F32),
        compiler_params=_cparams("arbitrary"),
        name="final_norm",
    )(x, w)


def _moe(h, x, g2, route, w1, w3, w2, tm, tf):
    n, d = x.shape
    n_asg = n * TOP_K
    flat_e = route[:, :TOP_K].astype(jnp.int32).reshape(-1)
    gates = route[:, TOP_K:2 * TOP_K]
    onehot = (flat_e[:, None] == jnp.arange(N_EXPERTS, dtype=jnp.int32)[None, :]).astype(jnp.int32)
    csum = jnp.cumsum(onehot, axis=0)
    counts = csum[-1]
    rank = jnp.take_along_axis(csum, flat_e[:, None], axis=1)[:, 0] - 1
    pcounts = (counts + tm - 1) // tm * tm
    pend = jnp.cumsum(pcounts)
    pstart = pend - pcounts
    dest = pstart[flat_e] + rank
    n_blocks = -(-n_asg // tm) + N_EXPERTS
    tok = jnp.arange(n_asg, dtype=jnp.int32) // TOP_K
    src_tok = jnp.zeros((n_blocks * tm,), jnp.int32).at[dest].set(tok)
    block_start = jnp.arange(n_blocks, dtype=pend.dtype) * tm
    block_e = jnp.minimum(jnp.sum((pend[None, :] <= block_start[:, None]).astype(jnp.int32), axis=1),
                          N_EXPERTS - 1)
    n_used = (pend[-1:] // tm).astype(jnp.int32)
    block_e = jnp.where(jnp.arange(n_blocks) < n_used[0], block_e, block_e[jnp.maximum(n_used[0] - 1, 0)])
    xs = jnp.take(h, src_tok, axis=0)
    ys = _expert_ffn(xs, block_e, n_used, w1, w3, w2, tm, tf)
    dst = dest.reshape(n, TOP_K)
    f = (gates[:, 0:1] * jnp.take(ys, dst[:, 0], axis=0).astype(F32)
         + gates[:, 1:2] * jnp.take(ys, dst[:, 1], axis=0).astype(F32))
    return x + g2 * f


def _prep_weights(w_in, gla_w_gk_up, gla_b_gk, hg_lb_logits):
    depth = w_in.shape[0]
    sizes = (HG_WIDTH,) * 5 + (GLA_QK, GLA_QK, GLA_WIDTH, GLA_WIDTH, GLA_GATE_RANK, GLA_GATE_RANK)
    offs = [0]
    for s in sizes:
        offs.append(offs[-1] + s)
    seg = lambda k: w_in[:, :, offs[k]:offs[k + 1]]
    hq, hf_f, hf_b, hi, hg, gq, gk, gv, gg, lr_f, lr_b = [seg(k) for k in range(11)]
    pad = jnp.zeros(w_in.shape[:2] + (LANES - 2 * GLA_GATE_RANK,), w_in.dtype)
    w_perm = jnp.concatenate([hq, hi, hg, gq, gk, gv, gg, hf_f, hf_b, lr_f, lr_b, pad], axis=-1).astype(BF16)

    r = GLA_GATE_RANK
    wup = jnp.zeros((depth, LANES, GLA_HEADS // 2, 2, LANES), F32)
    bup = jnp.zeros((depth, GLA_HEADS // 2, 2, LANES), F32)
    for dd in range(2):
        wup = wup.at[:, dd * r:(dd + 1) * r, :, dd, :].set(
            gla_w_gk_up[:, dd].reshape(depth, r, GLA_HEADS // 2, LANES))
        bup = bup.at[:, :, dd, :].set(gla_b_gk[:, dd].reshape(depth, GLA_HEADS // 2, LANES))
    wup = wup.reshape(depth, LANES, GLA_HEADS * LANES).astype(BF16)
    bup = bup.reshape(depth, 1, GLA_HEADS * LANES)

    lb = jnp.cumsum(jax.nn.softmax(hg_lb_logits.astype(F32), axis=0), axis=0)
    lb = lb - lb[0]
    return w_perm, wup, bup, lb


def _pick(pref, total):
    t = min(pref, total)
    while total % t:
        t //= 2
    return t


def _trunk(x3, mods, weights):
    (norm1_w, w_perm, lb, wup, bup, hg_norm_w, gla_norm_w, w_out, norm2_w, w_ff1, w_ff3, w_ff2,
     w_router, b_router, w_e1, w_e3, w_e2, final_norm_w) = weights
    nb, seq_len, d = x3.shape
    depth = norm1_w.shape[0]
    n = nb * seq_len
    x = x3.reshape(n, d)
    tm = _pick(512, seq_len)
    tm_ffn = _pick(1024, seq_len)
    tm_moe = _pick(1024, n * TOP_K)
    for l in range(depth):
        mod = mods[l]
        pa, pg = _in_proj(x, mod, norm1_w[l][None, :], w_perm[l], seq_len, tm)
        mh = _hg_mixer(pa, pg, lb[l], hg_norm_w[l][None, :], seq_len)
        mg = _gla_mixer(pa, pg, wup[l], bup[l], gla_norm_w[l][None, :], seq_len)
        m = l // 2
        if l % 2 == 0:
            x, h = _out_proj(x, mh, mg, mod, w_out[l], norm2_w[l][None, :], seq_len, tm)
            x = _dense_ffn(h, x, mod, w_ff1[m:m + 1], w_ff3[m:m + 1], w_ff2[m:m + 1], seq_len, tm_ffn,
                           _pick(256, w_ff1.shape[-1]))
        else:
            wr = jnp.pad(w_router[m], ((0, 0), (0, LANES - N_EXPERTS)))
            br = jnp.pad(b_router[m], (0, LANES - N_EXPERTS))[None, :]
            x, h, route = _out_proj(x, mh, mg, mod, w_out[l], norm2_w[l][None, :], seq_len, tm,
                                    router=(wr, br))
            g2 = jnp.repeat(mod[:, 5, :], seq_len, axis=0)
            x = _moe(h, x, g2, route, w_e1[m], w_e3[m], w_e2[m], tm_moe, _pick(512, w_e1.shape[-1]))
    return _final_norm(x, final_norm_w[None, :], tm).reshape(nb, seq_len, d)


def kernel(x_prompt, x_sample, c_prompt, c_sample, w_ada, b_ada, norm1_w, w_in, hg_lb_logits, gla_w_gk_up, gla_b_gk, hg_norm_w, gla_norm_w, w_out, norm2_w, w_ff1, w_ff3, w_ff2, w_router, b_router, w_e1, w_e3, w_e2, final_norm_w):
    depth, d = norm1_w.shape
    w_perm, wup, bup, lb = _prep_weights(w_in, gla_w_gk_up, gla_b_gk, hg_lb_logits)
    nbp = c_prompt.shape[0]
    mods = _ada_mods(jnp.concatenate([c_prompt, c_sample], axis=0), w_ada, b_ada)
    mods = mods.reshape(depth, mods.shape[1], 6, d)
    weights = (norm1_w, w_perm, lb, wup, bup, hg_norm_w, gla_norm_w, w_out.astype(BF16), norm2_w,
               w_ff1.astype(BF16), w_ff3.astype(BF16), w_ff2.astype(BF16), w_router, b_router,
               w_e1.astype(BF16), w_e3.astype(BF16), w_e2.astype(BF16), final_norm_w)
    y_prompt = _trunk(x_prompt, mods[:, :nbp], weights)
    y_sample = _trunk(x_sample, mods[:, nbp:], weights)
    return (y_prompt, y_sample)
```

```python
import functools

import jax
import jax.numpy as jnp
from jax import lax
from jax.experimental import pallas as pl
from jax.experimental.pallas import tpu as pltpu
from jax.experimental.pallas import tpu_sc as plsc

F32 = jnp.float32
BF16 = jnp.bfloat16

D_MODEL = 1024
HG_HEADS = 4
HG_DK = 128
HG_WIDTH = 512
GLA_HEADS = 4
GLA_DK = 64
GLA_DV = 128
GLA_QK = 256
GLA_WIDTH = 512
GLA_GATE_RANK = 16
GLA_GATE_NORMALIZER = 16.0
CHUNK = 64
N_EXPERTS = 8
TOP_K = 2
EPS = 1e-6
LANES = 128

PA_WIDTH = 3 * HG_WIDTH + 2 * GLA_QK + 2 * GLA_WIDTH
PG_WIDTH = 2 * HG_WIDTH + LANES
PK_WIDTH = 2 * HG_WIDTH
PL_WIDTH = (2 * HG_HEADS + 2 * (GLA_HEADS // 2)) * 2 * LANES

VMEM_LIMIT = 48 * 1024 * 1024
SC_CORES = 2
SC_SUBCORES = 16
MIXER_BLOCK_ROWS = 128


def _cparams(*sem):
    return pltpu.CompilerParams(dimension_semantics=sem, vmem_limit_bytes=VMEM_LIMIT)


def _sigmoid(x):
    return 1.0 / (1.0 + jnp.exp(-x))


def _split_bf16(a):
    hi = a.astype(BF16)
    lo = (a - hi.astype(F32)).astype(BF16)
    return hi, lo


def _dot(a, b):
    return jnp.dot(a, b, preferred_element_type=F32)


def _dot_nt(a, b):
    return lax.dot_general(a, b, (((1,), (1,)), ((), ())), preferred_element_type=F32)


def _dot_tn(a, b):
    return lax.dot_general(a, b, (((0,), (0,)), ((), ())), preferred_element_type=F32)


def _dot3(a, b_hi, b_lo):
    a_hi, a_lo = _split_bf16(a)
    return _dot(a_hi, b_hi) + (_dot(a_hi, b_lo) + _dot(a_lo, b_hi))


def _rms(x, w):
    ms = jnp.mean(x * x, axis=-1, keepdims=True)
    return x * lax.rsqrt(ms + EPS) * w


def _ada_kernel(c_ref, w_ref, b_ref, o_ref):
    c = c_ref[...]
    s = c * _sigmoid(c)
    w_hi, w_lo = _split_bf16(w_ref[0])
    o_ref[0] = _dot3(s, w_hi, w_lo) + b_ref[0]


def _ada_mods(c, w_ada, b_ada):
    depth, d, e = w_ada.shape
    nb = c.shape[0]
    tn = 1536
    return pl.pallas_call(
        _ada_kernel,
        grid=(depth, e // tn),
        in_specs=[
            pl.BlockSpec((nb, d), lambda l, j: (0, 0)),
            pl.BlockSpec((1, d, tn), lambda l, j: (l, 0, j)),
            pl.BlockSpec((1, 1, tn), lambda l, j: (l, 0, j)),
        ],
        out_specs=pl.BlockSpec((1, nb, tn), lambda l, j: (l, 0, j)),
        out_shape=jax.ShapeDtypeStruct((depth, nb, e), F32),
        compiler_params=_cparams("arbitrary", "arbitrary"),
        name="ada_mods",
    )(c, w_ada, b_ada.reshape(depth, 1, e))


def _hg_gate(z, lb):
    e = jnp.exp(-jnp.abs(z))
    r = 1.0 / (1.0 + e)
    er = e * r
    pos = z >= 0.0
    sig = jnp.where(pos, r, er)
    sig_neg = jnp.where(pos, er, r)
    oml = 1.0 - lb
    return oml * sig_neg, jnp.log(lb + oml * sig)


def _store_hi_lo(pl_ref, blk, logg):
    hi, lo = _split_bf16(logg)
    pl_ref[:, blk * 2 * LANES:blk * 2 * LANES + LANES] = hi
    pl_ref[:, blk * 2 * LANES + LANES:(blk + 1) * 2 * LANES] = lo


def _in_proj_kernel(x_ref, mod_ref, nw_ref, w_ref, lb_ref, wup_ref, bup_ref, pa_ref, pk_ref, pl_ref):
    m = mod_ref[0]
    h = _rms(x_ref[...], nw_ref[...]) * (1.0 + m[1:2, :]) + m[0:1, :]
    hb = h.astype(BF16)
    proj = lambda lo, hi: _dot(hb, w_ref[:, lo:hi])
    q = proj(0, HG_WIDTH)
    pa_ref[:, 0:HG_WIDTH] = (q * _sigmoid(q) * (HG_DK ** -0.5)).astype(BF16)
    for c in (1, 2, 4, 5):
        pa_ref[:, c * 512:(c + 1) * 512] = proj(c * 512, (c + 1) * 512).astype(BF16)
    qk = proj(3 * 512, 4 * 512)
    pa_ref[:, 3 * 512:3 * 512 + GLA_QK] = (qk[:, :GLA_QK] * (GLA_DK ** -0.5)).astype(BF16)
    pa_ref[:, 3 * 512 + GLA_QK:4 * 512] = qk[:, GLA_QK:].astype(BF16)
    lb = lb_ref[...]
    for d in range(2):
        z = proj(PA_WIDTH + d * HG_WIDTH, PA_WIDTH + (d + 1) * HG_WIDTH)
        k, logf = _hg_gate(z, lb[d:d + 1, :])
        pk_ref[:, d * HG_WIDTH:(d + 1) * HG_WIDTH] = k.astype(BF16)
        for hh in range(HG_HEADS):
            _store_hi_lo(pl_ref, d * HG_HEADS + hh, logf[:, hh * HG_DK:(hh + 1) * HG_DK])
    lr = proj(PA_WIDTH + 2 * HG_WIDTH, PA_WIDTH + PG_WIDTH).astype(BF16)
    zg = _dot(lr, wup_ref[...]) + bup_ref[...]
    logg = (jnp.minimum(zg, 0.0) - jnp.log(1.0 + jnp.exp(-jnp.abs(zg)))) * (1.0 / GLA_GATE_NORMALIZER)
    npair = GLA_HEADS // 2
    for p in range(npair):
        for d in range(2):
            src = (p * 2 + d) * LANES
            _store_hi_lo(pl_ref, 2 * HG_HEADS + d * npair + p, logg[:, src:src + LANES])


def _in_proj(x, mod, nw, w, lb, wup, bup, seq_len, tm):
    n, d = x.shape
    tps = seq_len // tm
    const = lambda shape: pl.BlockSpec(shape, lambda i: (0, 0))
    return pl.pallas_call(
        _in_proj_kernel,
        grid=(n // tm,),
        in_specs=[
            pl.BlockSpec((tm, d), lambda i: (i, 0)),
            pl.BlockSpec((1, 6, d), lambda i: (i // tps, 0, 0)),
            const((1, d)),
            const((d, PA_WIDTH + PG_WIDTH)),
            const((2, HG_WIDTH)),
            const((LANES, GLA_HEADS * LANES)),
            const((1, GLA_HEADS * LANES)),
        ],
        out_specs=[
            pl.BlockSpec((tm, PA_WIDTH), lambda i: (i, 0)),
            pl.BlockSpec((tm, PK_WIDTH), lambda i: (i, 0)),
            pl.BlockSpec((tm, PL_WIDTH), lambda i: (i, 0)),
        ],
        out_shape=[
            jax.ShapeDtypeStruct((n, PA_WIDTH), BF16),
            jax.ShapeDtypeStruct((n, PK_WIDTH), BF16),
            jax.ShapeDtypeStruct((n, PL_WIDTH), BF16),
        ],
        compiler_params=_cparams("arbitrary"),
        name="in_proj",
    )(x, mod, nw, w, lb, wup, bup)


def _block_masks(rows):
    r = jnp.arange(rows, dtype=jnp.int32)[:, None]
    c = jnp.arange(rows, dtype=jnp.int32)[None, :]
    same = (r // CHUNK) == (c // CHUNK)
    return jnp.stack([same & (c <= r), same & (c >= r)]).astype(F32)


def _chunk_slices(a, n_sub):
    return [a[c * CHUNK:(c + 1) * CHUNK] for c in range(n_sub)]


def _decay_factors(hl, tri, fwd):
    n_sub = hl.shape[0] // CHUNK
    w = hl.shape[1] // 2
    cs = _dot(tri, hl)
    b = cs[:, :w] + cs[:, w:]
    r = CHUNK // 2 if fwd else CHUNK // 2 - 1
    last = CHUNK - 1 if fwd else 0
    out = []
    for bc in _chunk_slices(b, n_sub):
        b_ref = bc[r:r + 1, :]
        b_last = bc[last:last + 1, :]
        out.append((jnp.exp(bc - b_ref), jnp.exp(b_ref - bc), jnp.exp(b_ref), jnp.exp(b_last - b_ref),
                    jnp.exp(b_last)))
    return out


def _stage_block(q, k, hl, tri, fwd, st_ref, dec_ref, d):
    n_sub = q.shape[0] // CHUNK
    fac = _decay_factors(hl, tri, fwd)
    for c in range(n_sub):
        sl = slice(c * CHUNK, (c + 1) * CHUNK)
        a, ainv, e_ref, e_last_ref, decay = fac[c]
        qa = q[sl] * a
        ka = k[sl] * ainv
        st_ref[0, d, sl, :] = qa.astype(BF16)
        st_ref[1, d, sl, :] = ka.astype(BF16)
        st_ref[2, d, sl, :] = (qa * e_ref).astype(BF16)
        st_ref[3, d, sl, :] = (ka * e_last_ref).astype(BF16)
        dec_ref[d, c * 8:(c + 1) * 8, :] = jnp.broadcast_to(decay, (8, decay.shape[1]))


def _mm_block(st_ref, dec_ref, d, v_heads, q_masks, maskf, s_ref, fwd):
    n_sub = st_ref.shape[2] // CHUNK
    qt = st_ref[0, d]
    kt = st_ref[1, d]
    mask = maskf > 0.5
    order = range(n_sub) if fwd else range(n_sub - 1, -1, -1)
    outs = []
    for h, v in enumerate(v_heads):
        hm = q_masks[h]
        sel = (lambda a: a) if hm is None else (lambda a, hm=hm: jnp.where(hm, a, jnp.zeros_like(a)))
        p = jnp.where(mask, _dot_nt(sel(qt), kt), 0.0).astype(BF16)
        o_intra = _chunk_slices(_dot(p, v), n_sub)
        vs = _chunk_slices(v, n_sub)
        s_t = s_ref[d, h]
        o = [None] * n_sub
        for c in order:
            sl = slice(c * CHUNK, (c + 1) * CHUNK)
            o[c] = o_intra[c] + _dot_nt(sel(st_ref[2, d, sl, :]), s_t.astype(BF16))
            s_t = dec_ref[d, c * 8:c * 8 + 1, :] * s_t + _dot_tn(vs[c], st_ref[3, d, sl, :])
        s_ref[d, h] = s_t
        outs.append(jnp.concatenate(o, axis=0))
    return outs


def _pipelined_steps(n_blocks, stage_fn, mm_fn, slots):
    n_pairs = n_blocks // 2

    def pair(p, fin0, fin1, last):
        stage_fn(2 * p + 1, slots[1])
        mm_fn(2 * p, slots[0], fin0)
        if not last:
            stage_fn(2 * p + 2, slots[0])
        mm_fn(2 * p + 1, slots[1], fin1)

    groups = []
    for p in range(n_pairs):
        key = (2 * p >= n_pairs, 2 * p + 1 >= n_pairs, p == n_pairs - 1)
        if groups and groups[-1][2] == key:
            groups[-1][1] = p + 1
        else:
            groups.append([p, p + 1, key])

    stage_fn(0, slots[0])
    for lo, hi, key in groups:
        if hi - lo == 1:
            pair(lo, *key)
        else:
            def body(p, carry, key=key):
                pair(p, *key)
                return carry
            lax.fori_loop(lo, hi, body, 0)


def _block_rows(step, d, n_blocks, blk_rows):
    blk = step if d == 0 else n_blocks - 1 - step
    if isinstance(blk, int):
        return pl.ds(blk * blk_rows, blk_rows)
    return pl.ds(pl.multiple_of(blk * blk_rows, blk_rows), blk_rows)


def _emit_block(outs, rows, fin, acc_ref, g_ref, nw, o_ref):
    for h, o in enumerate(outs):
        cols = slice(h * o.shape[1], (h + 1) * o.shape[1])
        if fin:
            g = g_ref[rows, cols].astype(F32)
            o_ref[rows, cols] = (_rms(o + acc_ref[rows, cols], nw) * (g * _sigmoid(g))).astype(o_ref.dtype)
        else:
            acc_ref[rows, cols] = o


def _hg_kernel(q_ref, i_ref, g_ref, kf_ref, kb_ref, hlf_ref, hlb_ref, nw_ref, mask_ref, o_ref,
               acc_ref, s_ref, st_a, dec_a, st_b, dec_b):
    blk_rows = mask_ref.shape[1]
    n_blocks = q_ref.shape[0] // blk_rows
    nw = nw_ref[...]
    k_refs = (kf_ref, kb_ref)
    hl_refs = (hlf_ref, hlb_ref)

    def stage_fn(step, slot):
        for d in range(2):
            rows = _block_rows(step, d, n_blocks, blk_rows)
            _stage_block(q_ref[rows, :].astype(F32), k_refs[d][rows, :].astype(F32), hl_refs[d][rows, :],
                         mask_ref[d].astype(BF16), d == 0, slot[0], slot[1], d)

    def mm_fn(step, slot, fin):
        for d in range(2):
            rows = _block_rows(step, d, n_blocks, blk_rows)
            outs = _mm_block(slot[0], slot[1], d, [i_ref[rows, :]], [None], mask_ref[d], s_ref, d == 0)
            _emit_block(outs, rows, fin, acc_ref, g_ref, nw, o_ref)

    s_ref[...] = jnp.zeros_like(s_ref)
    _pipelined_steps(n_blocks, stage_fn, mm_fn, ((st_a, dec_a), (st_b, dec_b)))


def _staging_scratch(blk_rows):
    slot = [pltpu.VMEM((4, 2, blk_rows, LANES), BF16), pltpu.VMEM((2, blk_rows // CHUNK * 8, LANES), F32)]
    return slot + slot


def _mixer_block_rows(seq_len):
    return min(MIXER_BLOCK_ROWS, seq_len // 2)


def _hg_mixer(pa, pk, plog, nw, seq_len):
    n = pa.shape[0]
    nh = HG_HEADS
    br = _mixer_block_rows(seq_len)
    blk = lambda off: pl.BlockSpec((seq_len, HG_DK), lambda b, h: (b, off + h))
    blk2 = lambda off: pl.BlockSpec((seq_len, 2 * LANES), lambda b, h: (b, off + h))
    return pl.pallas_call(
        _hg_kernel,
        grid=(n // seq_len, nh),
        in_specs=[
            blk(0), blk(nh), blk(2 * nh),
            blk(0), blk(nh),
            blk2(0), blk2(nh),
            pl.BlockSpec((1, HG_DK), lambda b, h: (0, 0)),
            pl.BlockSpec((2, br, br), lambda b, h: (0, 0, 0)),
        ],
        out_specs=pl.BlockSpec((seq_len, HG_DK), lambda b, h: (b, h)),
        out_shape=jax.ShapeDtypeStruct((n, HG_WIDTH), BF16),
        scratch_shapes=[pltpu.VMEM((seq_len, HG_DK), F32), pltpu.VMEM((2, 1, HG_DK, HG_DK), F32)]
        + _staging_scratch(br),
        compiler_params=_cparams("arbitrary", "arbitrary"),
        name="hg_mixer",
    )(pa, pa, pa, pk, pk, plog, plog, nw, _block_masks(br))


def _gla_kernel(q_ref, k_ref, v_ref, g_ref, hlf_ref, hlb_ref, nw_ref, mask_ref, o_ref,
                acc_ref, s_ref, st_a, dec_a, st_b, dec_b):
    blk_rows = mask_ref.shape[1]
    n_blocks = q_ref.shape[0] // blk_rows
    lane = lax.broadcasted_iota(jnp.int32, (1, LANES), 1)
    head_masks = [lane < GLA_DK, lane >= GLA_DK]
    nw = nw_ref[...]
    dv = GLA_DV
    hl_refs = (hlf_ref, hlb_ref)

    def stage_fn(step, slot):
        for d in range(2):
            rows = _block_rows(step, d, n_blocks, blk_rows)
            _stage_block(q_ref[rows, :].astype(F32), k_ref[rows, :].astype(F32), hl_refs[d][rows, :],
                         mask_ref[d].astype(BF16), d == 0, slot[0], slot[1], d)

    def mm_fn(step, slot, fin):
        for d in range(2):
            rows = _block_rows(step, d, n_blocks, blk_rows)
            v_heads = [v_ref[rows, hh * dv:(hh + 1) * dv] for hh in range(2)]
            outs = _mm_block(slot[0], slot[1], d, v_heads, head_masks, mask_ref[d], s_ref, d == 0)
            _emit_block(outs, rows, fin, acc_ref, g_ref, nw, o_ref)

    s_ref[...] = jnp.zeros_like(s_ref)
    _pipelined_steps(n_blocks, stage_fn, mm_fn, ((st_a, dec_a), (st_b, dec_b)))


def _gla_mixer(pa, plog, nw, seq_len):
    n = pa.shape[0]
    npair = GLA_HEADS // 2
    br = _mixer_block_rows(seq_len)
    return pl.pallas_call(
        _gla_kernel,
        grid=(n // seq_len, npair),
        in_specs=[
            pl.BlockSpec((seq_len, LANES), lambda b, p: (b, 12 + p)),
            pl.BlockSpec((seq_len, LANES), lambda b, p: (b, 14 + p)),
            pl.BlockSpec((seq_len, 2 * GLA_DV), lambda b, p: (b, 8 + p)),
            pl.BlockSpec((seq_len, 2 * GLA_DV), lambda b, p: (b, 10 + p)),
            pl.BlockSpec((seq_len, 2 * LANES), lambda b, p: (b, 2 * HG_HEADS + p)),
            pl.BlockSpec((seq_len, 2 * LANES), lambda b, p: (b, 2 * HG_HEADS + npair + p)),
            pl.BlockSpec((1, GLA_DV), lambda b, p: (0, 0)),
            pl.BlockSpec((2, br, br), lambda b, p: (0, 0, 0)),
        ],
        out_specs=pl.BlockSpec((seq_len, 2 * GLA_DV), lambda b, p: (b, p)),
        out_shape=jax.ShapeDtypeStruct((n, GLA_WIDTH), BF16),
        scratch_shapes=[pltpu.VMEM((seq_len, 2 * GLA_DV), F32), pltpu.VMEM((2, 2, GLA_DV, LANES), F32)]
        + _staging_scratch(br),
        compiler_params=_cparams("arbitrary", "arbitrary"),
        name="gla_mixer",
    )(pa, pa, pa, pa, plog, plog, nw, _block_masks(br))


def _out_proj_body(x_ref, mh_ref, mg_ref, mod_ref, w_ref, nw_ref):
    m = mod_ref[0]
    y = _dot(mh_ref[...], w_ref[0:HG_WIDTH, :]) + _dot(mg_ref[...], w_ref[HG_WIDTH:, :])
    x = x_ref[...] + m[2:3, :] * y
    h = _rms(x, nw_ref[...]) * (1.0 + m[4:5, :]) + m[3:4, :]
    return x, h


def _out_proj_kernel(x_ref, mh_ref, mg_ref, mod_ref, w_ref, nw_ref, xo_ref, h_ref):
    x, h = _out_proj_body(x_ref, mh_ref, mg_ref, mod_ref, w_ref, nw_ref)
    xo_ref[...] = x
    h_ref[...] = h.astype(BF16)


def _out_proj_route_kernel(x_ref, mh_ref, mg_ref, mod_ref, w_ref, nw_ref, wr_ref, br_ref,
                           xo_ref, h_ref, route_ref):
    x, h = _out_proj_body(x_ref, mh_ref, mg_ref, mod_ref, w_ref, nw_ref)
    xo_ref[...] = x
    h_ref[...] = h
    wr_hi, wr_lo = _split_bf16(wr_ref[...])
    logits = _dot3(h, wr_hi, wr_lo) + br_ref[...]
    lane = lax.broadcasted_iota(jnp.int32, logits.shape, 1).astype(F32)
    neg = jnp.float32(-jnp.inf)
    logits = jnp.where(lane < N_EXPERTS, logits, neg)
    m1 = jnp.max(logits, axis=-1, keepdims=True)
    i1 = jnp.min(jnp.where(logits == m1, lane, float(LANES)), axis=-1, keepdims=True)
    rest = jnp.where(lane == i1, neg, logits)
    m2 = jnp.max(rest, axis=-1, keepdims=True)
    i2 = jnp.min(jnp.where(rest == m2, lane, float(LANES)), axis=-1, keepdims=True)
    e2 = jnp.exp(m2 - m1)
    den = 1.0 / (1.0 + e2)
    route = jnp.where(lane == 0.0, i1,
                      jnp.where(lane == 1.0, i2,
                                jnp.where(lane == 2.0, den, jnp.where(lane == 3.0, e2 * den, 0.0))))
    route_ref[...] = route


def _out_proj(x, mh, mg, mod, w, nw, seq_len, tm, router=None):
    n, d = x.shape
    tps = seq_len // tm
    in_specs = [
        pl.BlockSpec((tm, d), lambda i: (i, 0)),
        pl.BlockSpec((tm, HG_WIDTH), lambda i: (i, 0)),
        pl.BlockSpec((tm, GLA_WIDTH), lambda i: (i, 0)),
        pl.BlockSpec((1, 6, d), lambda i: (i // tps, 0, 0)),
        pl.BlockSpec((d, d), lambda i: (0, 0)),
        pl.BlockSpec((1, d), lambda i: (0, 0)),
    ]
    out_specs = [pl.BlockSpec((tm, d), lambda i: (i, 0)), pl.BlockSpec((tm, d), lambda i: (i, 0))]
    out_shape = [jax.ShapeDtypeStruct((n, d), F32), jax.ShapeDtypeStruct((n, d), BF16)]
    args = [x, mh, mg, mod, w, nw]
    body = _out_proj_kernel
    if router is not None:
        in_specs += [pl.BlockSpec((d, LANES), lambda i: (0, 0)), pl.BlockSpec((1, LANES), lambda i: (0, 0))]
        out_specs.append(pl.BlockSpec((tm, LANES), lambda i: (i, 0)))
        out_shape[1] = jax.ShapeDtypeStruct((n, d), F32)
        out_shape.append(jax.ShapeDtypeStruct((n, LANES), F32))
        args += list(router)
        body = _out_proj_route_kernel
    return pl.pallas_call(
        body,
        grid=(n // tm,),
        in_specs=in_specs,
        out_specs=out_specs,
        out_shape=out_shape,
        compiler_params=_cparams("arbitrary"),
        name="out_proj",
    )(*args)


def _swiglu_step(x_ref, w1_ref, w3_ref, w2_ref, acc_ref):
    f = pl.program_id(1)
    xb = x_ref[...].astype(BF16)
    a = _dot(xb, w1_ref[0])
    b = _dot(xb, w3_ref[0])
    part = _dot((a * _sigmoid(a) * b).astype(BF16), w2_ref[0])

    @pl.when(f == 0)
    def _():
        acc_ref[...] = part

    @pl.when(f > 0)
    def _():
        acc_ref[...] += part


def _dense_ffn_kernel(h_ref, x_ref, mod_ref, w1_ref, w3_ref, w2_ref, o_ref, acc_ref):
    _swiglu_step(h_ref, w1_ref, w3_ref, w2_ref, acc_ref)

    @pl.when(pl.program_id(1) == pl.num_programs(1) - 1)
    def _():
        o_ref[...] = x_ref[...] + mod_ref[0][5:6, :] * acc_ref[...]


def _dense_ffn(h, x, mod, w1, w3, w2, seq_len, tm, tf):
    n, d = x.shape
    ff = w1.shape[-1]
    tps = seq_len // tm
    return pl.pallas_call(
        _dense_ffn_kernel,
        grid=(n // tm, ff // tf),
        in_specs=[
            pl.BlockSpec((tm, d), lambda i, f: (i, 0)),
            pl.BlockSpec((tm, d), lambda i, f: (i, 0)),
            pl.BlockSpec((1, 6, d), lambda i, f: (i // tps, 0, 0)),
            pl.BlockSpec((1, d, tf), lambda i, f: (0, 0, f)),
            pl.BlockSpec((1, d, tf), lambda i, f: (0, 0, f)),
            pl.BlockSpec((1, tf, d), lambda i, f: (0, f, 0)),
        ],
        out_specs=pl.BlockSpec((tm, d), lambda i, f: (i, 0)),
        out_shape=jax.ShapeDtypeStruct((n, d), F32),
        scratch_shapes=[pltpu.VMEM((tm, d), F32)],
        compiler_params=_cparams("arbitrary", "arbitrary"),
        name="dense_ffn",
    )(h, x, mod, w1, w3, w2)


def _expert_ffn_kernel(be_ref, nu_ref, xs_ref, w1_ref, w3_ref, w2_ref, o_ref, acc_ref):
    i = pl.program_id(0)
    last = pl.program_id(1) == pl.num_programs(1) - 1
    used = i < nu_ref[0]

    @pl.when(used)
    def _():
        _swiglu_step(xs_ref, w1_ref, w3_ref, w2_ref, acc_ref)

    @pl.when(jnp.logical_and(used, last))
    def _():
        o_ref[...] = acc_ref[...].astype(o_ref.dtype)

    @pl.when(jnp.logical_and(jnp.logical_not(used), last))
    def _():
        o_ref[...] = jnp.zeros_like(o_ref)


def _expert_ffn(xs, block_e, n_used, w1, w3, w2, tm, tf):
    rows, d = xs.shape
    ff = w1.shape[-1]
    nf = ff // tf

    def row_idx(i, f, be, nu):
        return (jnp.minimum(i, nu[0] - 1), 0)

    def fsel(i, f, nu):
        return jnp.where(i < nu[0], f, nf - 1)

    grid_spec = pltpu.PrefetchScalarGridSpec(
        num_scalar_prefetch=2,
        grid=(rows // tm, nf),
        in_specs=[
            pl.BlockSpec((tm, d), row_idx),
            pl.BlockSpec((1, d, tf), lambda i, f, be, nu: (be[i], 0, fsel(i, f, nu))),
            pl.BlockSpec((1, d, tf), lambda i, f, be, nu: (be[i], 0, fsel(i, f, nu))),
            pl.BlockSpec((1, tf, d), lambda i, f, be, nu: (be[i], fsel(i, f, nu), 0)),
        ],
        out_specs=pl.BlockSpec((tm, d), lambda i, f, be, nu: (i, 0)),
        scratch_shapes=[pltpu.VMEM((tm, d), F32)],
    )
    return pl.pallas_call(
        _expert_ffn_kernel,
        grid_spec=grid_spec,
        out_shape=jax.ShapeDtypeStruct((rows, d), F32),
        compiler_params=_cparams("arbitrary", "arbitrary"),
        name="expert_ffn",
    )(block_e, n_used, xs, w1, w3, w2)


def _gather_rows(table, idx):
    n_rows = idx.shape[0]
    d = table.shape[1]
    n_workers = SC_CORES * SC_SUBCORES
    per_worker = n_rows // n_workers
    chunk = next(c for c in (64, 32, 16, 8) if per_worker % c == 0)
    n_chunks = per_worker // chunk
    mesh = plsc.VectorSubcoreMesh(core_axis_name="c", subcore_axis_name="s")

    @functools.partial(
        pl.kernel, mesh=mesh, out_type=jax.ShapeDtypeStruct((n_rows, d), table.dtype),
        scratch_types=[pltpu.VMEM((chunk,), jnp.int32), pltpu.VMEM((chunk, d), table.dtype),
                       pltpu.SemaphoreType.DMA])
    def gather_kernel(table_hbm, idx_hbm, out_hbm, idx_v, rows_v, sem):
        worker = lax.axis_index("s") * SC_CORES + lax.axis_index("c")
        base = worker * per_worker

        @pl.loop(0, n_chunks)
        def _(j):
            off = pl.multiple_of(base + j * chunk, chunk)
            pltpu.sync_copy(idx_hbm.at[pl.ds(off, chunk)], idx_v)
            pltpu.async_copy(table_hbm.at[idx_v], rows_v, sem).wait()
            pltpu.sync_copy(rows_v, out_hbm.at[pl.ds(off, chunk)])

    return gather_kernel(table, idx)


def _combine_kernel(x_ref, y_ref, route_ref, mod_ref, o_ref):
    d = x_ref.shape[1]
    r = route_ref[...]
    f = r[:, TOP_K:TOP_K + 1] * y_ref[:, 0:d] + r[:, TOP_K + 1:TOP_K + 2] * y_ref[:, d:2 * d]
    o_ref[...] = x_ref[...] + mod_ref[0][5:6, :] * f


def _combine_norm_kernel(x_ref, y_ref, route_ref, mod_ref, nw_ref, o_ref):
    d = x_ref.shape[1]
    r = route_ref[...]
    f = r[:, TOP_K:TOP_K + 1] * y_ref[:, 0:d] + r[:, TOP_K + 1:TOP_K + 2] * y_ref[:, d:2 * d]
    o_ref[...] = _rms(x_ref[...] + mod_ref[0][5:6, :] * f, nw_ref[...])


def _combine(x, y2, route, mod, seq_len, tm, final_nw=None):
    n, d = x.shape
    tps = seq_len // tm
    in_specs = [
        pl.BlockSpec((tm, d), lambda i: (i, 0)),
        pl.BlockSpec((tm, TOP_K * d), lambda i: (i, 0)),
        pl.BlockSpec((tm, LANES), lambda i: (i, 0)),
        pl.BlockSpec((1, 6, d), lambda i: (i // tps, 0, 0)),
    ]
    args = [x, y2, route, mod]
    body = _combine_kernel
    if final_nw is not None:
        in_specs.append(pl.BlockSpec((1, d), lambda i: (0, 0)))
        args.append(final_nw)
        body = _combine_norm_kernel
    return pl.pallas_call(
        body,
        grid=(n // tm,),
        in_specs=in_specs,
        out_specs=pl.BlockSpec((tm, d), lambda i: (i, 0)),
        out_shape=jax.ShapeDtypeStruct((n, d), F32),
        compiler_params=_cparams("arbitrary"),
        name="moe_combine",
    )(*args)


def _final_norm_kernel(x_ref, w_ref, o_ref):
    o_ref[...] = _rms(x_ref[...], w_ref[...])


def _final_norm(x, w, tm):
    n, d = x.shape
    return pl.pallas_call(
        _final_norm_kernel,
        grid=(n // tm,),
        in_specs=[pl.BlockSpec((tm, d), lambda i: (i, 0)), pl.BlockSpec((1, d), lambda i: (0, 0))],
        out_specs=pl.BlockSpec((tm, d), lambda i: (i, 0)),
        out_shape=jax.ShapeDtypeStruct((n, d), F32),
        compiler_params=_cparams("arbitrary"),
        name="final_norm",
    )(x, w)


def _moe(h, x, mod, route, w1, w3, w2, seq_len, tm, tf, tm_c, final_nw):
    n, d = x.shape
    n_asg = n * TOP_K
    flat_e = route[:, :TOP_K].astype(jnp.int32).reshape(-1)
    onehot = (flat_e[:, None] == jnp.arange(N_EXPERTS, dtype=jnp.int32)[None, :]).astype(jnp.int32)
    csum = jnp.cumsum(onehot, axis=0)
    counts = csum[-1]
    rank = jnp.take_along_axis(csum, flat_e[:, None], axis=1)[:, 0] - 1
    pcounts = (counts + tm - 1) // tm * tm
    pend = jnp.cumsum(pcounts)
    pstart = pend - pcounts
    dest = pstart[flat_e] + rank
    n_blocks = -(-n_asg // tm) + N_EXPERTS
    tok = jnp.arange(n_asg, dtype=jnp.int32) // TOP_K
    src_tok = jnp.zeros((n_blocks * tm,), jnp.int32).at[dest].set(tok)
    block_start = jnp.arange(n_blocks, dtype=pend.dtype) * tm
    block_e = jnp.minimum(jnp.sum((pend[None, :] <= block_start[:, None]).astype(jnp.int32), axis=1),
                          N_EXPERTS - 1)
    n_used = (pend[-1:] // tm).astype(jnp.int32)
    block_e = jnp.where(jnp.arange(n_blocks) < n_used[0], block_e, block_e[jnp.maximum(n_used[0] - 1, 0)])
    xs = _gather_rows(h, src_tok)
    ys = _expert_ffn(xs, block_e, n_used, w1, w3, w2, tm, tf)
    y2 = _gather_rows(ys, dest).reshape(n, TOP_K * d)
    return _combine(x, y2, route, mod, seq_len, tm_c, final_nw)


def _prep_weights(w_in, gla_w_gk_up, gla_b_gk, hg_lb_logits):
    depth = w_in.shape[0]
    sizes = (HG_WIDTH,) * 5 + (GLA_QK, GLA_QK, GLA_WIDTH, GLA_WIDTH, GLA_GATE_RANK, GLA_GATE_RANK)
    offs = [0]
    for s in sizes:
        offs.append(offs[-1] + s)
    seg = lambda k: w_in[:, :, offs[k]:offs[k + 1]]
    hq, hf_f, hf_b, hi, hg, gq, gk, gv, gg, lr_f, lr_b = [seg(k) for k in range(11)]
    pad = jnp.zeros(w_in.shape[:2] + (LANES - 2 * GLA_GATE_RANK,), w_in.dtype)
    w_perm = jnp.concatenate([hq, hi, hg, gq, gk, gv, gg, hf_f, hf_b, lr_f, lr_b, pad], axis=-1).astype(BF16)

    r = GLA_GATE_RANK
    wup = jnp.zeros((depth, LANES, GLA_HEADS // 2, 2, LANES), F32)
    bup = jnp.zeros((depth, GLA_HEADS // 2, 2, LANES), F32)
    for dd in range(2):
        wup = wup.at[:, dd * r:(dd + 1) * r, :, dd, :].set(
            gla_w_gk_up[:, dd].reshape(depth, r, GLA_HEADS // 2, LANES))
        bup = bup.at[:, :, dd, :].set(gla_b_gk[:, dd].reshape(depth, GLA_HEADS // 2, LANES))
    wup = wup.reshape(depth, LANES, GLA_HEADS * LANES).astype(BF16)
    bup = bup.reshape(depth, 1, GLA_HEADS * LANES)

    lb = jnp.cumsum(jax.nn.softmax(hg_lb_logits.astype(F32), axis=0), axis=0)
    lb = lb - lb[0]
    return w_perm, wup, bup, lb


def _pick(pref, total):
    t = min(pref, total)
    while total % t:
        t //= 2
    return t


def _trunk(x3, mods, weights):
    (norm1_w, w_perm, lb, wup, bup, hg_norm_w, gla_norm_w, w_out, norm2_w, w_ff1, w_ff3, w_ff2,
     w_router, b_router, w_e1, w_e3, w_e2, final_norm_w) = weights
    nb, seq_len, d = x3.shape
    depth = norm1_w.shape[0]
    n = nb * seq_len
    x = x3.reshape(n, d)
    tm = _pick(512, seq_len)
    tm_ffn = _pick(1024, seq_len)
    tm_moe = _pick(1024, n * TOP_K)
    for l in range(depth):
        mod = mods[l]
        pa, pk, plog = _in_proj(x, mod, norm1_w[l][None, :], w_perm[l], lb[l], wup[l], bup[l], seq_len, tm)
        mh = _hg_mixer(pa, pk, plog, hg_norm_w[l][None, :], seq_len)
        mg = _gla_mixer(pa, plog, gla_norm_w[l][None, :], seq_len)
        m = l // 2
        if l % 2 == 0:
            x, h = _out_proj(x, mh, mg, mod, w_out[l], norm2_w[l][None, :], seq_len, tm)
            x = _dense_ffn(h, x, mod, w_ff1[m:m + 1], w_ff3[m:m + 1], w_ff2[m:m + 1], seq_len, tm_ffn,
                           _pick(256, w_ff1.shape[-1]))
        else:
            wr = jnp.pad(w_router[m], ((0, 0), (0, LANES - N_EXPERTS)))
            br = jnp.pad(b_router[m], (0, LANES - N_EXPERTS))[None, :]
            x, h, route = _out_proj(x, mh, mg, mod, w_out[l], norm2_w[l][None, :], seq_len, tm,
                                    router=(wr, br))
            final_nw = final_norm_w[None, :] if l == depth - 1 else None
            x = _moe(h, x, mod, route, w_e1[m], w_e3[m], w_e2[m], seq_len, tm_moe,
                     _pick(512, w_e1.shape[-1]), tm, final_nw)
    if depth % 2 == 1:
        x = _final_norm(x, final_norm_w[None, :], tm)
    return x.reshape(nb, seq_len, d)


def kernel(x_prompt, x_sample, c_prompt, c_sample, w_ada, b_ada, norm1_w, w_in, hg_lb_logits, gla_w_gk_up, gla_b_gk, hg_norm_w, gla_norm_w, w_out, norm2_w, w_ff1, w_ff3, w_ff2, w_router, b_router, w_e1, w_e3, w_e2, final_norm_w):
    depth, d = norm1_w.shape
    w_perm, wup, bup, lb = _prep_weights(w_in, gla_w_gk_up, gla_b_gk, hg_lb_logits)
    nbp = c_prompt.shape[0]
    mods = _ada_mods(jnp.concatenate([c_prompt, c_sample], axis=0), w_ada, b_ada)
    mods = mods.reshape(depth, mods.shape[1], 6, d)
    weights = (norm1_w, w_perm, lb, wup, bup, hg_norm_w, gla_norm_w, w_out.astype(BF16), norm2_w,
               w_ff1.astype(BF16), w_ff3.astype(BF16), w_ff2.astype(BF16), w_router, b_router,
               w_e1.astype(BF16), w_e3.astype(BF16), w_e2.astype(BF16), final_norm_w)
    y_prompt = _trunk(x_prompt, mods[:, :nbp], weights)
    y_sample = _trunk(x_sample, mods[:, nbp:], weights)
    return (y_prompt, y_sample)
```

```python
import functools

import jax
import jax.numpy as jnp
from jax import lax
from jax.experimental import pallas as pl
from jax.experimental.pallas import tpu as pltpu
from jax.experimental.pallas import tpu_sc as plsc

F32 = jnp.float32
BF16 = jnp.bfloat16

D_MODEL = 1024
HG_HEADS = 4
HG_DK = 128
HG_WIDTH = 512
GLA_HEADS = 4
GLA_DK = 64
GLA_DV = 128
GLA_QK = 256
GLA_WIDTH = 512
GLA_GATE_RANK = 16
GLA_GATE_NORMALIZER = 16.0
CHUNK = 64
N_EXPERTS = 8
TOP_K = 2
EPS = 1e-6
LANES = 128

PA_WIDTH = 3 * HG_WIDTH + 2 * GLA_QK + 2 * GLA_WIDTH
PG_WIDTH = 2 * HG_WIDTH + LANES
PK_WIDTH = 2 * HG_WIDTH
PL_WIDTH = (2 * HG_HEADS + 2 * (GLA_HEADS // 2)) * 2 * LANES

VMEM_LIMIT = 48 * 1024 * 1024
SC_CORES = 2
SC_SUBCORES = 16
MIXER_BLOCK_ROWS = 128
MIXER_VMEM_LIMIT = 56 * 1024 * 1024
MIXER_VMEM_BUDGET = 46 * 1024 * 1024


def _cparams(*sem):
    return pltpu.CompilerParams(dimension_semantics=sem, vmem_limit_bytes=VMEM_LIMIT)


def _sigmoid(x):
    return 1.0 / (1.0 + jnp.exp(-x))


def _split_bf16(a):
    hi = a.astype(BF16)
    lo = (a - hi.astype(F32)).astype(BF16)
    return hi, lo


def _dot(a, b):
    return jnp.dot(a, b, preferred_element_type=F32)


def _dot_nt(a, b):
    return lax.dot_general(a, b, (((1,), (1,)), ((), ())), preferred_element_type=F32)


def _dot_tn(a, b):
    return lax.dot_general(a, b, (((0,), (0,)), ((), ())), preferred_element_type=F32)


def _dot3(a, b_hi, b_lo):
    a_hi, a_lo = _split_bf16(a)
    return _dot(a_hi, b_hi) + (_dot(a_hi, b_lo) + _dot(a_lo, b_hi))


def _rms(x, w):
    ms = jnp.mean(x * x, axis=-1, keepdims=True)
    return x * lax.rsqrt(ms + EPS) * w


def _ada_kernel(c_ref, w_ref, b_ref, o_ref):
    c = c_ref[...]
    s = c * _sigmoid(c)
    w_hi, w_lo = _split_bf16(w_ref[0])
    o_ref[0] = _dot3(s, w_hi, w_lo) + b_ref[0]


def _ada_mods(c, w_ada, b_ada):
    depth, d, e = w_ada.shape
    nb = c.shape[0]
    tn = 1536
    return pl.pallas_call(
        _ada_kernel,
        grid=(depth, e // tn),
        in_specs=[
            pl.BlockSpec((nb, d), lambda l, j: (0, 0)),
            pl.BlockSpec((1, d, tn), lambda l, j: (l, 0, j)),
            pl.BlockSpec((1, 1, tn), lambda l, j: (l, 0, j)),
        ],
        out_specs=pl.BlockSpec((1, nb, tn), lambda l, j: (l, 0, j)),
        out_shape=jax.ShapeDtypeStruct((depth, nb, e), F32),
        compiler_params=_cparams("arbitrary", "arbitrary"),
        name="ada_mods",
    )(c, w_ada, b_ada.reshape(depth, 1, e))


def _hg_gate(z, lb):
    e = jnp.exp(-jnp.abs(z))
    r = 1.0 / (1.0 + e)
    er = e * r
    pos = z >= 0.0
    sig = jnp.where(pos, r, er)
    sig_neg = jnp.where(pos, er, r)
    oml = 1.0 - lb
    return oml * sig_neg, jnp.log(lb + oml * sig)


def _store_hi_lo(pl_ref, blk, logg):
    hi, lo = _split_bf16(logg)
    pl_ref[:, blk * 2 * LANES:blk * 2 * LANES + LANES] = hi
    pl_ref[:, blk * 2 * LANES + LANES:(blk + 1) * 2 * LANES] = lo


def _in_proj_kernel(x_ref, mod_ref, nw_ref, w_ref, lb_ref, wup_ref, bup_ref, pa_ref, pk_ref, pl_ref):
    m = mod_ref[0]
    h = _rms(x_ref[...], nw_ref[...]) * (1.0 + m[1:2, :]) + m[0:1, :]
    hb = h.astype(BF16)
    proj = lambda lo, hi: _dot(hb, w_ref[:, lo:hi])
    q = proj(0, HG_WIDTH)
    pa_ref[:, 0:HG_WIDTH] = (q * _sigmoid(q) * (HG_DK ** -0.5)).astype(BF16)
    for c in (1, 2, 4, 5):
        pa_ref[:, c * 512:(c + 1) * 512] = proj(c * 512, (c + 1) * 512).astype(BF16)
    qk = proj(3 * 512, 4 * 512)
    pa_ref[:, 3 * 512:3 * 512 + GLA_QK] = (qk[:, :GLA_QK] * (GLA_DK ** -0.5)).astype(BF16)
    pa_ref[:, 3 * 512 + GLA_QK:4 * 512] = qk[:, GLA_QK:].astype(BF16)
    lb = lb_ref[...]
    for d in range(2):
        z = proj(PA_WIDTH + d * HG_WIDTH, PA_WIDTH + (d + 1) * HG_WIDTH)
        k, logf = _hg_gate(z, lb[d:d + 1, :])
        pk_ref[:, d * HG_WIDTH:(d + 1) * HG_WIDTH] = k.astype(BF16)
        for hh in range(HG_HEADS):
            _store_hi_lo(pl_ref, d * HG_HEADS + hh, logf[:, hh * HG_DK:(hh + 1) * HG_DK])
    lr = proj(PA_WIDTH + 2 * HG_WIDTH, PA_WIDTH + PG_WIDTH).astype(BF16)
    zg = _dot(lr, wup_ref[...]) + bup_ref[...]
    logg = (jnp.minimum(zg, 0.0) - jnp.log(1.0 + jnp.exp(-jnp.abs(zg)))) * (1.0 / GLA_GATE_NORMALIZER)
    npair = GLA_HEADS // 2
    for p in range(npair):
        for d in range(2):
            src = (p * 2 + d) * LANES
            _store_hi_lo(pl_ref, 2 * HG_HEADS + d * npair + p, logg[:, src:src + LANES])


def _in_proj(x, mod, nw, w, lb, wup, bup, seq_len, tm):
    n, d = x.shape
    tps = seq_len // tm
    const = lambda shape: pl.BlockSpec(shape, lambda i: (0, 0))
    return pl.pallas_call(
        _in_proj_kernel,
        grid=(n // tm,),
        in_specs=[
            pl.BlockSpec((tm, d), lambda i: (i, 0)),
            pl.BlockSpec((1, 6, d), lambda i: (i // tps, 0, 0)),
            const((1, d)),
            const((d, PA_WIDTH + PG_WIDTH)),
            const((2, HG_WIDTH)),
            const((LANES, GLA_HEADS * LANES)),
            const((1, GLA_HEADS * LANES)),
        ],
        out_specs=[
            pl.BlockSpec((tm, PA_WIDTH), lambda i: (i, 0)),
            pl.BlockSpec((tm, PK_WIDTH), lambda i: (i, 0)),
            pl.BlockSpec((tm, PL_WIDTH), lambda i: (i, 0)),
        ],
        out_shape=[
            jax.ShapeDtypeStruct((n, PA_WIDTH), BF16),
            jax.ShapeDtypeStruct((n, PK_WIDTH), BF16),
            jax.ShapeDtypeStruct((n, PL_WIDTH), BF16),
        ],
        compiler_params=_cparams("arbitrary"),
        name="in_proj",
    )(x, mod, nw, w, lb, wup, bup)


def _block_masks(rows):
    r = jnp.arange(rows, dtype=jnp.int32)[:, None]
    c = jnp.arange(rows, dtype=jnp.int32)[None, :]
    same = (r // CHUNK) == (c // CHUNK)
    return jnp.stack([same & (c <= r), same & (c >= r)]).astype(F32)


def _chunk_slices(a, n_sub):
    return [a[c * CHUNK:(c + 1) * CHUNK] for c in range(n_sub)]


def _decay_factors(cs, fwd):
    n_sub = cs.shape[0] // CHUNK
    w = cs.shape[1] // 2
    b = cs[:, :w] + cs[:, w:]
    r = CHUNK // 2 if fwd else CHUNK // 2 - 1
    last = CHUNK - 1 if fwd else 0
    out = []
    for bc in _chunk_slices(b, n_sub):
        b_ref = bc[r:r + 1, :]
        b_last = bc[last:last + 1, :]
        out.append((jnp.exp(bc - b_ref), jnp.exp(b_ref - bc), jnp.exp(b_ref), jnp.exp(b_last - b_ref),
                    jnp.exp(b_last)))
    return out


def _stage_chains(chains):
    sums = [_dot(tri, hl) for _, _, hl, tri, _, _, _, _ in chains]
    for cs, (q_fn, k_fn, _, _, fwd, st_ref, dec_ref, d) in zip(sums, chains):
        fac = _decay_factors(cs, fwd)
        q = q_fn()
        k = k_fn()
        for c in range(len(fac)):
            sl = slice(c * CHUNK, (c + 1) * CHUNK)
            a, ainv, e_ref, e_last_ref, decay = fac[c]
            qa = q[sl] * a
            ka = k[sl] * ainv
            st_ref[0, d, sl, :] = qa.astype(BF16)
            st_ref[1, d, sl, :] = ka.astype(BF16)
            st_ref[2, d, sl, :] = (qa * e_ref).astype(BF16)
            st_ref[3, d, sl, :] = (ka * e_last_ref).astype(BF16)
            dec_ref[d, c * 8:(c + 1) * 8, :] = jnp.broadcast_to(decay, (8, decay.shape[1]))


def _mm_chains(chains):
    n_sub = chains[0][0].shape[2] // CHUNK
    heads = []
    for ci, (st_ref, dec_ref, d, v_heads, q_masks, maskf, s_ref, fwd) in enumerate(chains):
        for h, v in enumerate(v_heads):
            hm = q_masks[h]
            sel = (lambda a: a) if hm is None else (lambda a, hm=hm: jnp.where(hm, a, jnp.zeros_like(a)))
            heads.append(dict(ci=ci, st=st_ref, dec=dec_ref, d=d, h=h, v=v, vs=_chunk_slices(v, n_sub),
                              sel=sel, maskf=maskf, s_ref=s_ref, fwd=fwd))
    for e in heads:
        e["sc"] = _dot_nt(e["sel"](e["st"][0, e["d"]]), e["st"][1, e["d"]])
    for e in heads:
        e["ds"] = [_dot_tn(e["vs"][c], e["st"][3, e["d"], c * CHUNK:(c + 1) * CHUNK, :]) for c in range(n_sub)]
    for e in heads:
        p = jnp.where(e["maskf"] > 0.5, e["sc"], 0.0).astype(BF16)
        e["oi"] = _chunk_slices(_dot(p, e["v"]), n_sub)
    outs = [[] for _ in chains]
    for e in heads:
        d, st_ref, dec_ref = e["d"], e["st"], e["dec"]
        s_t = e["s_ref"][d, e["h"]]
        o = [None] * n_sub
        for c in (range(n_sub) if e["fwd"] else range(n_sub - 1, -1, -1)):
            sl = slice(c * CHUNK, (c + 1) * CHUNK)
            o[c] = e["oi"][c] + _dot_nt(e["sel"](st_ref[2, d, sl, :]), s_t.astype(BF16))
            s_t = dec_ref[d, c * 8:c * 8 + 1, :] * s_t + e["ds"][c]
        e["s_ref"][d, e["h"]] = s_t
        outs[e["ci"]].append(jnp.concatenate(o, axis=0))
    return outs


def _pipelined_steps(n_blocks, stage_fn, mm_fn, slots):
    n_pairs = n_blocks // 2

    def pair(p, fin0, fin1, last):
        stage_fn(2 * p + 1, slots[1])
        mm_fn(2 * p, slots[0], fin0)
        if not last:
            stage_fn(2 * p + 2, slots[0])
        mm_fn(2 * p + 1, slots[1], fin1)

    groups = []
    for p in range(n_pairs):
        key = (2 * p >= n_pairs, 2 * p + 1 >= n_pairs, p == n_pairs - 1)
        if groups and groups[-1][2] == key:
            groups[-1][1] = p + 1
        else:
            groups.append([p, p + 1, key])

    stage_fn(0, slots[0])
    for lo, hi, key in groups:
        if hi - lo == 1:
            pair(lo, *key)
        else:
            def body(p, carry, key=key):
                pair(p, *key)
                return carry
            lax.fori_loop(lo, hi, body, 0)


def _block_rows(step, d, n_blocks, blk_rows):
    blk = step if d == 0 else n_blocks - 1 - step
    if isinstance(blk, int):
        return pl.ds(blk * blk_rows, blk_rows)
    return pl.ds(pl.multiple_of(blk * blk_rows, blk_rows), blk_rows)


def _emit_block(outs, rows, col0, fin, acc_ref, g_ref, nw, o_ref):
    for h, o in enumerate(outs):
        cols = slice(col0 + h * o.shape[1], col0 + (h + 1) * o.shape[1])
        if fin:
            g = g_ref[rows, cols].astype(F32)
            o_ref[rows, cols] = (_rms(o + acc_ref[rows, cols], nw) * (g * _sigmoid(g))).astype(o_ref.dtype)
        else:
            acc_ref[rows, cols] = o


def _mixer_kernel(heads_per_stream, q_ref, kf_ref, kb_ref, v_ref, g_ref, hlf_ref, hlb_ref, nw_ref, mask_ref,
                  o_ref, acc_ref, s_ref, st_a, dec_a, st_b, dec_b):
    blk_rows = mask_ref.shape[1]
    n_blocks = q_ref.shape[0] // blk_rows
    n_streams = q_ref.shape[1] // LANES
    dv = v_ref.shape[1] // (n_streams * heads_per_stream)
    nw = nw_ref[...]
    k_refs = (kf_ref, kb_ref)
    hl_refs = (hlf_ref, hlb_ref)
    if heads_per_stream == 1:
        q_masks = [None]
    else:
        lane = lax.broadcasted_iota(jnp.int32, (1, LANES), 1)
        q_masks = [lane < GLA_DK, lane >= GLA_DK]

    def stage_fn(step, slot):
        chains = []
        for d in range(2):
            tri = mask_ref[d].astype(BF16)
            rows = _block_rows(step, d, n_blocks, blk_rows)
            for s in range(n_streams):
                cs = slice(s * LANES, (s + 1) * LANES)
                chains.append((lambda rows=rows, cs=cs: q_ref[rows, cs].astype(F32),
                               lambda rows=rows, cs=cs, d=d: k_refs[d][rows, cs].astype(F32),
                               hl_refs[d][rows, 2 * s * LANES:2 * (s + 1) * LANES], tri, d == 0,
                               slot[0].at[s], slot[1].at[s], d))
        _stage_chains(chains)

    def mm_fn(step, slot, fin):
        chains, where = [], []
        for d in range(2):
            rows = _block_rows(step, d, n_blocks, blk_rows)
            for s in range(n_streams):
                col0 = s * heads_per_stream * dv
                v_heads = [v_ref[rows, col0 + h * dv:col0 + (h + 1) * dv] for h in range(heads_per_stream)]
                chains.append((slot[0].at[s], slot[1].at[s], d, v_heads, q_masks, mask_ref[d],
                               s_ref.at[s], d == 0))
                where.append((rows, col0))
        for outs, (rows, col0) in zip(_mm_chains(chains), where):
            _emit_block(outs, rows, col0, fin, acc_ref, g_ref, nw, o_ref)

    s_ref[...] = jnp.zeros_like(s_ref)
    _pipelined_steps(n_blocks, stage_fn, mm_fn, ((st_a, dec_a), (st_b, dec_b)))


def _mixer_call(name, arrays, col_units, heads_per_stream, n_streams, n_groups, dv, out_width, nw, seq_len):
    n = arrays[0].shape[0]
    br = _mixer_block_rows(seq_len)
    vw = n_streams * heads_per_stream * dv
    widths = (n_streams * LANES,) * 3 + (vw, vw) + (2 * n_streams * LANES,) * 2
    in_specs = [pl.BlockSpec((seq_len, w), lambda b, j, off=off: (b, off + j))
                for w, off in zip(widths, col_units)]
    in_specs += [pl.BlockSpec((1, dv), lambda b, j: (0, 0)), pl.BlockSpec((2, br, br), lambda b, j: (0, 0, 0))]
    slot = [pltpu.VMEM((n_streams, 4, 2, br, LANES), BF16),
            pltpu.VMEM((n_streams, 2, br // CHUNK * 8, LANES), F32)]
    return pl.pallas_call(
        functools.partial(_mixer_kernel, heads_per_stream),
        grid=(n // seq_len, n_groups),
        in_specs=in_specs,
        out_specs=pl.BlockSpec((seq_len, vw), lambda b, j: (b, j)),
        out_shape=jax.ShapeDtypeStruct((n, out_width), BF16),
        scratch_shapes=[pltpu.VMEM((seq_len, vw), F32),
                        pltpu.VMEM((n_streams, 2, heads_per_stream, dv, LANES), F32)] + slot + slot,
        compiler_params=pltpu.CompilerParams(dimension_semantics=("arbitrary", "arbitrary"),
                                             vmem_limit_bytes=MIXER_VMEM_LIMIT),
        name=name,
    )(*arrays, nw, _block_masks(br))


def _mixer_block_rows(seq_len):
    return min(MIXER_BLOCK_ROWS, seq_len // 2)


def _mixer_streams(seq_len, bytes_per_row_per_stream):
    return 2 if 2 * seq_len * bytes_per_row_per_stream <= MIXER_VMEM_BUDGET else 1


def _hg_mixer(pa, pk, plog, nw, seq_len):
    ns = _mixer_streams(seq_len, (5 * 2 + 2 * 4 + 2) * 2 * LANES + 4 * LANES)
    ng = HG_HEADS // ns
    units = (0, 0, HG_HEADS // ns, HG_HEADS // ns, 2 * HG_HEADS // ns, 0, HG_HEADS // ns)
    return _mixer_call("hg_mixer", (pa, pk, pk, pa, pa, plog, plog), units, 1, ns, ng, HG_DK, HG_WIDTH,
                       nw, seq_len)


def _gla_mixer(pa, plog, nw, seq_len):
    ns = _mixer_streams(seq_len, (3 * 2 + 2 * 4 + 2 * 4 + 4) * 2 * LANES + 8 * LANES)
    npair = GLA_HEADS // 2
    ng = npair // ns
    units = (12 // ns, 14 // ns, 14 // ns, 8 // ns, 10 // ns, 2 * HG_HEADS // ns, (2 * HG_HEADS + npair) // ns)
    return _mixer_call("gla_mixer", (pa, pa, pa, pa, pa, plog, plog), units, 2, ns, ng, GLA_DV, GLA_WIDTH,
                       nw, seq_len)


def _out_proj_body(x_ref, mh_ref, mg_ref, mod_ref, w_ref, nw_ref):
    m = mod_ref[0]
    y = _dot(mh_ref[...], w_ref[0:HG_WIDTH, :]) + _dot(mg_ref[...], w_ref[HG_WIDTH:, :])
    x = x_ref[...] + m[2:3, :] * y
    h = _rms(x, nw_ref[...]) * (1.0 + m[4:5, :]) + m[3:4, :]
    return x, h


def _out_proj_kernel(x_ref, mh_ref, mg_ref, mod_ref, w_ref, nw_ref, xo_ref, h_ref):
    x, h = _out_proj_body(x_ref, mh_ref, mg_ref, mod_ref, w_ref, nw_ref)
    xo_ref[...] = x
    h_ref[...] = h.astype(BF16)


def _out_proj_route_kernel(x_ref, mh_ref, mg_ref, mod_ref, w_ref, nw_ref, wr_ref, br_ref,
                           xo_ref, h_ref, route_ref):
    x, h = _out_proj_body(x_ref, mh_ref, mg_ref, mod_ref, w_ref, nw_ref)
    xo_ref[...] = x
    h_ref[...] = h
    wr_hi, wr_lo = _split_bf16(wr_ref[...])
    logits = _dot3(h, wr_hi, wr_lo) + br_ref[...]
    lane = lax.broadcasted_iota(jnp.int32, logits.shape, 1).astype(F32)
    neg = jnp.float32(-jnp.inf)
    logits = jnp.where(lane < N_EXPERTS, logits, neg)
    m1 = jnp.max(logits, axis=-1, keepdims=True)
    i1 = jnp.min(jnp.where(logits == m1, lane, float(LANES)), axis=-1, keepdims=True)
    rest = jnp.where(lane == i1, neg, logits)
    m2 = jnp.max(rest, axis=-1, keepdims=True)
    i2 = jnp.min(jnp.where(rest == m2, lane, float(LANES)), axis=-1, keepdims=True)
    e2 = jnp.exp(m2 - m1)
    den = 1.0 / (1.0 + e2)
    route = jnp.where(lane == 0.0, i1,
                      jnp.where(lane == 1.0, i2,
                                jnp.where(lane == 2.0, den, jnp.where(lane == 3.0, e2 * den, 0.0))))
    route_ref[...] = route


def _out_proj(x, mh, mg, mod, w, nw, seq_len, tm, router=None):
    n, d = x.shape
    tps = seq_len // tm
    in_specs = [
        pl.BlockSpec((tm, d), lambda i: (i, 0)),
        pl.BlockSpec((tm, HG_WIDTH), lambda i: (i, 0)),
        pl.BlockSpec((tm, GLA_WIDTH), lambda i: (i, 0)),
        pl.BlockSpec((1, 6, d), lambda i: (i // tps, 0, 0)),
        pl.BlockSpec((d, d), lambda i: (0, 0)),
        pl.BlockSpec((1, d), lambda i: (0, 0)),
    ]
    out_specs = [pl.BlockSpec((tm, d), lambda i: (i, 0)), pl.BlockSpec((tm, d), lambda i: (i, 0))]
    out_shape = [jax.ShapeDtypeStruct((n, d), F32), jax.ShapeDtypeStruct((n, d), BF16)]
    args = [x, mh, mg, mod, w, nw]
    body = _out_proj_kernel
    if router is not None:
        in_specs += [pl.BlockSpec((d, LANES), lambda i: (0, 0)), pl.BlockSpec((1, LANES), lambda i: (0, 0))]
        out_specs.append(pl.BlockSpec((tm, LANES), lambda i: (i, 0)))
        out_shape[1] = jax.ShapeDtypeStruct((n, d), F32)
        out_shape.append(jax.ShapeDtypeStruct((n, LANES), F32))
        args += list(router)
        body = _out_proj_route_kernel
    return pl.pallas_call(
        body,
        grid=(n // tm,),
        in_specs=in_specs,
        out_specs=out_specs,
        out_shape=out_shape,
        compiler_params=_cparams("arbitrary"),
        name="out_proj",
    )(*args)


def _swiglu_step(x_ref, w1_ref, w3_ref, w2_ref, acc_ref):
    f = pl.program_id(1)
    xb = x_ref[...].astype(BF16)
    a = _dot(xb, w1_ref[0])
    b = _dot(xb, w3_ref[0])
    part = _dot((a * _sigmoid(a) * b).astype(BF16), w2_ref[0])

    @pl.when(f == 0)
    def _():
        acc_ref[...] = part

    @pl.when(f > 0)
    def _():
        acc_ref[...] += part


def _dense_ffn_kernel(h_ref, x_ref, mod_ref, w1_ref, w3_ref, w2_ref, o_ref, acc_ref):
    _swiglu_step(h_ref, w1_ref, w3_ref, w2_ref, acc_ref)

    @pl.when(pl.program_id(1) == pl.num_programs(1) - 1)
    def _():
        o_ref[...] = x_ref[...] + mod_ref[0][5:6, :] * acc_ref[...]


def _dense_ffn(h, x, mod, w1, w3, w2, seq_len, tm, tf):
    n, d = x.shape
    ff = w1.shape[-1]
    tps = seq_len // tm
    return pl.pallas_call(
        _dense_ffn_kernel,
        grid=(n // tm, ff // tf),
        in_specs=[
            pl.BlockSpec((tm, d), lambda i, f: (i, 0)),
            pl.BlockSpec((tm, d), lambda i, f: (i, 0)),
            pl.BlockSpec((1, 6, d), lambda i, f: (i // tps, 0, 0)),
            pl.BlockSpec((1, d, tf), lambda i, f: (0, 0, f)),
            pl.BlockSpec((1, d, tf), lambda i, f: (0, 0, f)),
            pl.BlockSpec((1, tf, d), lambda i, f: (0, f, 0)),
        ],
        out_specs=pl.BlockSpec((tm, d), lambda i, f: (i, 0)),
        out_shape=jax.ShapeDtypeStruct((n, d), F32),
        scratch_shapes=[pltpu.VMEM((tm, d), F32)],
        compiler_params=_cparams("arbitrary", "arbitrary"),
        name="dense_ffn",
    )(h, x, mod, w1, w3, w2)


def _expert_ffn_kernel(be_ref, nu_ref, xs_ref, w1_ref, w3_ref, w2_ref, o_ref, acc_ref):
    i = pl.program_id(0)
    last = pl.program_id(1) == pl.num_programs(1) - 1
    used = i < nu_ref[0]

    @pl.when(used)
    def _():
        _swiglu_step(xs_ref, w1_ref, w3_ref, w2_ref, acc_ref)

    @pl.when(jnp.logical_and(used, last))
    def _():
        o_ref[...] = acc_ref[...].astype(o_ref.dtype)

    @pl.when(jnp.logical_and(jnp.logical_not(used), last))
    def _():
        o_ref[...] = jnp.zeros_like(o_ref)


def _expert_ffn(xs, block_e, n_used, w1, w3, w2, tm, tf):
    rows, d = xs.shape
    ff = w1.shape[-1]
    nf = ff // tf

    def row_idx(i, f, be, nu):
        return (jnp.minimum(i, nu[0] - 1), 0)

    def fsel(i, f, nu):
        return jnp.where(i < nu[0], f, nf - 1)

    wmode = dict(pipeline_mode=pl.Buffered(1)) if nf == 1 else {}
    grid_spec = pltpu.PrefetchScalarGridSpec(
        num_scalar_prefetch=2,
        grid=(rows // tm, nf),
        in_specs=[
            pl.BlockSpec((tm, d), row_idx),
            pl.BlockSpec((1, d, tf), lambda i, f, be, nu: (be[i], 0, fsel(i, f, nu)), **wmode),
            pl.BlockSpec((1, d, tf), lambda i, f, be, nu: (be[i], 0, fsel(i, f, nu)), **wmode),
            pl.BlockSpec((1, tf, d), lambda i, f, be, nu: (be[i], fsel(i, f, nu), 0), **wmode),
        ],
        out_specs=pl.BlockSpec((tm, d), lambda i, f, be, nu: (i, 0)),
        scratch_shapes=[pltpu.VMEM((tm, d), F32)],
    )
    return pl.pallas_call(
        _expert_ffn_kernel,
        grid_spec=grid_spec,
        out_shape=jax.ShapeDtypeStruct((rows, d), F32),
        compiler_params=_cparams("arbitrary", "arbitrary"),
        name="expert_ffn",
    )(block_e, n_used, xs, w1, w3, w2)


def _gather_rows(table, idx):
    n_rows = idx.shape[0]
    d = table.shape[1]
    n_workers = SC_CORES * SC_SUBCORES
    per_worker = n_rows // n_workers
    chunk = next(c for c in (64, 32, 16, 8) if per_worker % c == 0)
    n_chunks = per_worker // chunk
    mesh = plsc.VectorSubcoreMesh(core_axis_name="c", subcore_axis_name="s")

    @functools.partial(
        pl.kernel, mesh=mesh, out_type=jax.ShapeDtypeStruct((n_rows, d), table.dtype),
        scratch_types=[pltpu.VMEM((chunk,), jnp.int32), pltpu.VMEM((chunk, d), table.dtype),
                       pltpu.SemaphoreType.DMA])
    def gather_kernel(table_hbm, idx_hbm, out_hbm, idx_v, rows_v, sem):
        worker = lax.axis_index("s") * SC_CORES + lax.axis_index("c")
        base = worker * per_worker

        @pl.loop(0, n_chunks)
        def _(j):
            off = pl.multiple_of(base + j * chunk, chunk)
            pltpu.sync_copy(idx_hbm.at[pl.ds(off, chunk)], idx_v)
            pltpu.async_copy(table_hbm.at[idx_v], rows_v, sem).wait()
            pltpu.sync_copy(rows_v, out_hbm.at[pl.ds(off, chunk)])

    return gather_kernel(table, idx)


def _combine_body(x_ref, ya_ref, yb_ref, route_ref, mod_ref):
    r = route_ref[...]
    f = r[:, TOP_K:TOP_K + 1] * ya_ref[...] + r[:, TOP_K + 1:TOP_K + 2] * yb_ref[...]
    return x_ref[...] + mod_ref[0][5:6, :] * f


def _combine_kernel(x_ref, ya_ref, yb_ref, route_ref, mod_ref, o_ref):
    o_ref[...] = _combine_body(x_ref, ya_ref, yb_ref, route_ref, mod_ref)


def _combine_norm_kernel(x_ref, ya_ref, yb_ref, route_ref, mod_ref, nw_ref, o_ref):
    o_ref[...] = _rms(_combine_body(x_ref, ya_ref, yb_ref, route_ref, mod_ref), nw_ref[...])


def _combine(x, y2, route, mod, seq_len, tm, final_nw=None):
    n, d = x.shape
    tps = seq_len // tm
    nt = n // tm
    in_specs = [
        pl.BlockSpec((tm, d), lambda i: (i, 0)),
        pl.BlockSpec((tm, d), lambda i: (i, 0)),
        pl.BlockSpec((tm, d), lambda i: (i + nt, 0)),
        pl.BlockSpec((tm, LANES), lambda i: (i, 0)),
        pl.BlockSpec((1, 6, d), lambda i: (i // tps, 0, 0)),
    ]
    args = [x, y2, y2, route, mod]
    body = _combine_kernel
    if final_nw is not None:
        in_specs.append(pl.BlockSpec((1, d), lambda i: (0, 0)))
        args.append(final_nw)
        body = _combine_norm_kernel
    return pl.pallas_call(
        body,
        grid=(n // tm,),
        in_specs=in_specs,
        out_specs=pl.BlockSpec((tm, d), lambda i: (i, 0)),
        out_shape=jax.ShapeDtypeStruct((n, d), F32),
        compiler_params=_cparams("arbitrary"),
        name="moe_combine",
    )(*args)


def _final_norm_kernel(x_ref, w_ref, o_ref):
    o_ref[...] = _rms(x_ref[...], w_ref[...])


def _final_norm(x, w, tm):
    n, d = x.shape
    return pl.pallas_call(
        _final_norm_kernel,
        grid=(n // tm,),
        in_specs=[pl.BlockSpec((tm, d), lambda i: (i, 0)), pl.BlockSpec((1, d), lambda i: (0, 0))],
        out_specs=pl.BlockSpec((tm, d), lambda i: (i, 0)),
        out_shape=jax.ShapeDtypeStruct((n, d), F32),
        compiler_params=_cparams("arbitrary"),
        name="final_norm",
    )(x, w)


def _moe(h, x, mod, route, w1, w3, w2, seq_len, tm, tf, tm_c, final_nw):
    n, d = x.shape
    n_asg = n * TOP_K
    flat_e = route[:, :TOP_K].astype(jnp.int32).reshape(-1)
    onehot = (flat_e[:, None] == jnp.arange(N_EXPERTS, dtype=jnp.int32)[None, :]).astype(jnp.int32)
    csum = jnp.cumsum(onehot, axis=0)
    counts = csum[-1]
    rank = jnp.take_along_axis(csum, flat_e[:, None], axis=1)[:, 0] - 1
    pcounts = (counts + tm - 1) // tm * tm
    pend = jnp.cumsum(pcounts)
    pstart = pend - pcounts
    dest = pstart[flat_e] + rank
    n_blocks = -(-n_asg // tm) + N_EXPERTS
    tok = jnp.arange(n_asg, dtype=jnp.int32) // TOP_K
    src_tok = jnp.zeros((n_blocks * tm,), jnp.int32).at[dest].set(tok)
    block_start = jnp.arange(n_blocks, dtype=pend.dtype) * tm
    block_e = jnp.minimum(jnp.sum((pend[None, :] <= block_start[:, None]).astype(jnp.int32), axis=1),
                          N_EXPERTS - 1)
    n_used = (pend[-1:] // tm).astype(jnp.int32)
    block_e = jnp.where(jnp.arange(n_blocks) < n_used[0], block_e, block_e[jnp.maximum(n_used[0] - 1, 0)])
    xs = _gather_rows(h, src_tok)
    ys = _expert_ffn(xs, block_e, n_used, w1, w3, w2, tm, tf)
    y2 = _gather_rows(ys, dest.reshape(n, TOP_K).T.reshape(-1))
    return _combine(x, y2, route, mod, seq_len, tm_c, final_nw)


def _prep_weights(w_in, gla_w_gk_up, gla_b_gk, hg_lb_logits):
    depth = w_in.shape[0]
    sizes = (HG_WIDTH,) * 5 + (GLA_QK, GLA_QK, GLA_WIDTH, GLA_WIDTH, GLA_GATE_RANK, GLA_GATE_RANK)
    offs = [0]
    for s in sizes:
        offs.append(offs[-1] + s)
    seg = lambda k: w_in[:, :, offs[k]:offs[k + 1]]
    hq, hf_f, hf_b, hi, hg, gq, gk, gv, gg, lr_f, lr_b = [seg(k) for k in range(11)]
    pad = jnp.zeros(w_in.shape[:2] + (LANES - 2 * GLA_GATE_RANK,), w_in.dtype)
    w_perm = jnp.concatenate([hq, hi, hg, gq, gk, gv, gg, hf_f, hf_b, lr_f, lr_b, pad], axis=-1).astype(BF16)

    r = GLA_GATE_RANK
    wup = jnp.zeros((depth, LANES, GLA_HEADS // 2, 2, LANES), F32)
    bup = jnp.zeros((depth, GLA_HEADS // 2, 2, LANES), F32)
    for dd in range(2):
        wup = wup.at[:, dd * r:(dd + 1) * r, :, dd, :].set(
            gla_w_gk_up[:, dd].reshape(depth, r, GLA_HEADS // 2, LANES))
        bup = bup.at[:, :, dd, :].set(gla_b_gk[:, dd].reshape(depth, GLA_HEADS // 2, LANES))
    wup = wup.reshape(depth, LANES, GLA_HEADS * LANES).astype(BF16)
    bup = bup.reshape(depth, 1, GLA_HEADS * LANES)

    lb = jnp.cumsum(jax.nn.softmax(hg_lb_logits.astype(F32), axis=0), axis=0)
    lb = lb - lb[0]
    return w_perm, wup, bup, lb


def _pick(pref, total):
    t = min(pref, total)
    while total % t:
        t //= 2
    return t


def _trunk(x3, mods, weights):
    (norm1_w, w_perm, lb, wup, bup, hg_norm_w, gla_norm_w, w_out, norm2_w, w_ff1, w_ff3, w_ff2,
     w_router, b_router, w_e1, w_e3, w_e2, final_norm_w) = weights
    nb, seq_len, d = x3.shape
    depth = norm1_w.shape[0]
    n = nb * seq_len
    x = x3.reshape(n, d)
    tm = _pick(512, seq_len)
    tm_ffn = _pick(512, seq_len)
    tm_moe = _pick(512, n * TOP_K)
    for l in range(depth):
        mod = mods[l]
        pa, pk, plog = _in_proj(x, mod, norm1_w[l][None, :], w_perm[l], lb[l], wup[l], bup[l], seq_len, tm)
        mh = _hg_mixer(pa, pk, plog, hg_norm_w[l][None, :], seq_len)
        mg = _gla_mixer(pa, plog, gla_norm_w[l][None, :], seq_len)
        m = l // 2
        if l % 2 == 0:
            x, h = _out_proj(x, mh, mg, mod, w_out[l], norm2_w[l][None, :], seq_len, tm)
            x = _dense_ffn(h, x, mod, w_ff1[m:m + 1], w_ff3[m:m + 1], w_ff2[m:m + 1], seq_len, tm_ffn,
                           w_ff1.shape[-1])
        else:
            wr = jnp.pad(w_router[m], ((0, 0), (0, LANES - N_EXPERTS)))
            br = jnp.pad(b_router[m], (0, LANES - N_EXPERTS))[None, :]
            x, h, route = _out_proj(x, mh, mg, mod, w_out[l], norm2_w[l][None, :], seq_len, tm,
                                    router=(wr, br))
            final_nw = final_norm_w[None, :] if l == depth - 1 else None
            x = _moe(h, x, mod, route, w_e1[m], w_e3[m], w_e2[m], seq_len, tm_moe,
                     w_e1.shape[-1], tm, final_nw)
    if depth % 2 == 1:
        x = _final_norm(x, final_norm_w[None, :], tm)
    return x.reshape(nb, seq_len, d)


def kernel(x_prompt, x_sample, c_prompt, c_sample, w_ada, b_ada, norm1_w, w_in, hg_lb_logits, gla_w_gk_up, gla_b_gk, hg_norm_w, gla_norm_w, w_out, norm2_w, w_ff1, w_ff3, w_ff2, w_router, b_router, w_e1, w_e3, w_e2, final_norm_w):
    depth, d = norm1_w.shape
    w_perm, wup, bup, lb = _prep_weights(w_in, gla_w_gk_up, gla_b_gk, hg_lb_logits)
    nbp = c_prompt.shape[0]
    mods = _ada_mods(jnp.concatenate([c_prompt, c_sample], axis=0), w_ada, b_ada)
    mods = mods.reshape(depth, mods.shape[1], 6, d)
    weights = (norm1_w, w_perm, lb, wup, bup, hg_norm_w, gla_norm_w, w_out.astype(BF16), norm2_w,
               w_ff1.astype(BF16), w_ff3.astype(BF16), w_ff2.astype(BF16), w_router, b_router,
               w_e1.astype(BF16), w_e3.astype(BF16), w_e2.astype(BF16), final_norm_w)
    y_prompt = _trunk(x_prompt, mods[:, :nbp], weights)
    y_sample = _trunk(x_sample, mods[:, nbp:], weights)
    return (y_prompt, y_sample)
```

```python
import functools

import jax
import jax.numpy as jnp
from jax import lax
from jax.experimental import pallas as pl
from jax.experimental.pallas import tpu as pltpu
from jax.experimental.pallas import tpu_sc as plsc

F32 = jnp.float32
BF16 = jnp.bfloat16

D_MODEL = 1024
HG_HEADS = 4
HG_DK = 128
HG_WIDTH = 512
GLA_HEADS = 4
GLA_DK = 64
GLA_DV = 128
GLA_QK = 256
GLA_WIDTH = 512
GLA_GATE_RANK = 16
GLA_GATE_NORMALIZER = 16.0
CHUNK = 64
N_EXPERTS = 8
TOP_K = 2
EPS = 1e-6
LANES = 128

PA_WIDTH = 3 * HG_WIDTH + 2 * GLA_QK + 2 * GLA_WIDTH
PG_WIDTH = 2 * HG_WIDTH + LANES
PK_WIDTH = 2 * HG_WIDTH
PL_WIDTH = (2 * HG_HEADS + 2 * (GLA_HEADS // 2)) * 2 * LANES

VMEM_LIMIT = 48 * 1024 * 1024
SC_CORES = 2
SC_SUBCORES = 16
MIXER_BLOCK_ROWS = 128
MIXER_VMEM_LIMIT = 56 * 1024 * 1024
MIXER_VMEM_BUDGET = 46 * 1024 * 1024


def _cparams(*sem):
    return pltpu.CompilerParams(dimension_semantics=sem, vmem_limit_bytes=VMEM_LIMIT)


def _sigmoid(x):
    return 1.0 / (1.0 + jnp.exp(-x))


def _split_bf16(a):
    hi = a.astype(BF16)
    lo = (a - hi.astype(F32)).astype(BF16)
    return hi, lo


def _dot(a, b):
    return jnp.dot(a, b, preferred_element_type=F32)


def _dot_nt(a, b):
    return lax.dot_general(a, b, (((1,), (1,)), ((), ())), preferred_element_type=F32)


def _dot_tn(a, b):
    return lax.dot_general(a, b, (((0,), (0,)), ((), ())), preferred_element_type=F32)


def _dot3(a, b_hi, b_lo):
    a_hi, a_lo = _split_bf16(a)
    return _dot(a_hi, b_hi) + (_dot(a_hi, b_lo) + _dot(a_lo, b_hi))


def _rms(x, w):
    ms = jnp.mean(x * x, axis=-1, keepdims=True)
    return x * lax.rsqrt(ms + EPS) * w


def _ada_kernel(c_ref, w_ref, b_ref, o_ref):
    c = c_ref[...]
    s = c * _sigmoid(c)
    w_hi, w_lo = _split_bf16(w_ref[0])
    o_ref[0] = _dot3(s, w_hi, w_lo) + b_ref[0]


def _ada_mods(c, w_ada, b_ada):
    depth, d, e = w_ada.shape
    nb = c.shape[0]
    tn = 1536
    return pl.pallas_call(
        _ada_kernel,
        grid=(depth, e // tn),
        in_specs=[
            pl.BlockSpec((nb, d), lambda l, j: (0, 0)),
            pl.BlockSpec((1, d, tn), lambda l, j: (l, 0, j)),
            pl.BlockSpec((1, 1, tn), lambda l, j: (l, 0, j)),
        ],
        out_specs=pl.BlockSpec((1, nb, tn), lambda l, j: (l, 0, j)),
        out_shape=jax.ShapeDtypeStruct((depth, nb, e), F32),
        compiler_params=_cparams("arbitrary", "arbitrary"),
        name="ada_mods",
    )(c, w_ada, b_ada.reshape(depth, 1, e))


def _hg_gate(z, lb):
    e = jnp.exp(-jnp.abs(z))
    r = 1.0 / (1.0 + e)
    er = e * r
    pos = z >= 0.0
    sig = jnp.where(pos, r, er)
    sig_neg = jnp.where(pos, er, r)
    oml = 1.0 - lb
    return oml * sig_neg, jnp.log(lb + oml * sig)


def _store_hi_lo(pl_ref, blk, logg):
    hi, lo = _split_bf16(logg)
    pl_ref[:, blk * 2 * LANES:blk * 2 * LANES + LANES] = hi
    pl_ref[:, blk * 2 * LANES + LANES:(blk + 1) * 2 * LANES] = lo


def _in_proj_kernel(x_ref, mod_ref, nw_ref, w_ref, lb_ref, wup_ref, bup_ref, pa_ref, pk_ref, pl_ref):
    m = mod_ref[0]
    h = _rms(x_ref[...], nw_ref[...]) * (1.0 + m[1:2, :]) + m[0:1, :]
    hb = h.astype(BF16)
    proj = lambda lo, hi: _dot(hb, w_ref[:, lo:hi])

    def plain(c):
        pa_ref[:, c * 512:(c + 1) * 512] = proj(c * 512, (c + 1) * 512).astype(BF16)

    lb = lb_ref[...]
    for d in range(2):
        z = proj(PA_WIDTH + d * HG_WIDTH, PA_WIDTH + (d + 1) * HG_WIDTH)
        k, logf = _hg_gate(z, lb[d:d + 1, :])
        pk_ref[:, d * HG_WIDTH:(d + 1) * HG_WIDTH] = k.astype(BF16)
        for hh in range(HG_HEADS):
            _store_hi_lo(pl_ref, d * HG_HEADS + hh, logf[:, hh * HG_DK:(hh + 1) * HG_DK])
        plain(1 + d)
    lr = proj(PA_WIDTH + 2 * HG_WIDTH, PA_WIDTH + PG_WIDTH).astype(BF16)
    zg = _dot(lr, wup_ref[...]) + bup_ref[...]
    logg = (jnp.minimum(zg, 0.0) - jnp.log(1.0 + jnp.exp(-jnp.abs(zg)))) * (1.0 / GLA_GATE_NORMALIZER)
    npair = GLA_HEADS // 2
    for p in range(npair):
        for d in range(2):
            src = (p * 2 + d) * LANES
            _store_hi_lo(pl_ref, 2 * HG_HEADS + d * npair + p, logg[:, src:src + LANES])
    plain(4)
    q = proj(0, HG_WIDTH)
    pa_ref[:, 0:HG_WIDTH] = (q * _sigmoid(q) * (HG_DK ** -0.5)).astype(BF16)
    plain(5)
    qk = proj(3 * 512, 4 * 512)
    pa_ref[:, 3 * 512:3 * 512 + GLA_QK] = (qk[:, :GLA_QK] * (GLA_DK ** -0.5)).astype(BF16)
    pa_ref[:, 3 * 512 + GLA_QK:4 * 512] = qk[:, GLA_QK:].astype(BF16)


def _in_proj(x, mod, nw, w, lb, wup, bup, seq_len, tm):
    n, d = x.shape
    tps = seq_len // tm
    const = lambda shape: pl.BlockSpec(shape, lambda i: (0, 0))
    return pl.pallas_call(
        _in_proj_kernel,
        grid=(n // tm,),
        in_specs=[
            pl.BlockSpec((tm, d), lambda i: (i, 0)),
            pl.BlockSpec((1, 6, d), lambda i: (i // tps, 0, 0)),
            const((1, d)),
            const((d, PA_WIDTH + PG_WIDTH)),
            const((2, HG_WIDTH)),
            const((LANES, GLA_HEADS * LANES)),
            const((1, GLA_HEADS * LANES)),
        ],
        out_specs=[
            pl.BlockSpec((tm, PA_WIDTH), lambda i: (i, 0)),
            pl.BlockSpec((tm, PK_WIDTH), lambda i: (i, 0)),
            pl.BlockSpec((tm, PL_WIDTH), lambda i: (i, 0)),
        ],
        out_shape=[
            jax.ShapeDtypeStruct((n, PA_WIDTH), BF16),
            jax.ShapeDtypeStruct((n, PK_WIDTH), BF16),
            jax.ShapeDtypeStruct((n, PL_WIDTH), BF16),
        ],
        compiler_params=_cparams("arbitrary"),
        name="in_proj",
    )(x, mod, nw, w, lb, wup, bup)


def _block_masks(rows):
    r = jnp.arange(rows, dtype=jnp.int32)[:, None]
    c = jnp.arange(rows, dtype=jnp.int32)[None, :]
    same = (r // CHUNK) == (c // CHUNK)
    return jnp.stack([same & (c <= r), same & (c >= r)]).astype(F32)


def _chunk_slices(a, n_sub):
    return [a[c * CHUNK:(c + 1) * CHUNK] for c in range(n_sub)]


def _decay_factors(cs, fwd):
    n_sub = cs.shape[0] // CHUNK
    w = cs.shape[1] // 2
    b = cs[:, :w] + cs[:, w:]
    r = CHUNK // 2 if fwd else CHUNK // 2 - 1
    last = CHUNK - 1 if fwd else 0
    out = []
    for bc in _chunk_slices(b, n_sub):
        b_ref = bc[r:r + 1, :]
        b_last = bc[last:last + 1, :]
        out.append((jnp.exp(bc - b_ref), jnp.exp(b_ref - bc), jnp.exp(b_ref), jnp.exp(b_last - b_ref),
                    jnp.exp(b_last)))
    return out


def _stage_chains(chains):
    sums = [_dot(tri, hl) for _, _, hl, tri, _, _, _, _ in chains]
    for cs, (q_fn, k_fn, _, _, fwd, st_ref, dec_ref, d) in zip(sums, chains):
        fac = _decay_factors(cs, fwd)
        q = q_fn()
        k = k_fn()
        for c in range(len(fac)):
            sl = slice(c * CHUNK, (c + 1) * CHUNK)
            a, ainv, e_ref, e_last_ref, decay = fac[c]
            qa = q[sl] * a
            ka = k[sl] * ainv
            st_ref[0, d, sl, :] = qa.astype(BF16)
            st_ref[1, d, sl, :] = ka.astype(BF16)
            st_ref[2, d, sl, :] = (qa * e_ref).astype(BF16)
            st_ref[3, d, sl, :] = (ka * e_last_ref).astype(BF16)
            dec_ref[d, c * 8:(c + 1) * 8, :] = jnp.broadcast_to(decay, (8, decay.shape[1]))


def _mm_chains(chains):
    n_sub = chains[0][0].shape[2] // CHUNK
    heads = []
    for ci, (st_ref, dec_ref, d, v_heads, q_masks, maskf, s_ref, fwd) in enumerate(chains):
        for h, v in enumerate(v_heads):
            hm = q_masks[h]
            sel = (lambda a: a) if hm is None else (lambda a, hm=hm: jnp.where(hm, a, jnp.zeros_like(a)))
            heads.append(dict(ci=ci, st=st_ref, dec=dec_ref, d=d, h=h, v=v, vs=_chunk_slices(v, n_sub),
                              sel=sel, maskf=maskf, s_ref=s_ref, fwd=fwd))
    for e in heads:
        e["sc"] = _dot_nt(e["sel"](e["st"][0, e["d"]]), e["st"][1, e["d"]])
    for e in heads:
        e["ds"] = [_dot_tn(e["vs"][c], e["st"][3, e["d"], c * CHUNK:(c + 1) * CHUNK, :]) for c in range(n_sub)]
    for e in heads:
        p = jnp.where(e["maskf"] > 0.5, e["sc"], 0.0).astype(BF16)
        e["oi"] = _chunk_slices(_dot(p, e["v"]), n_sub)
    outs = [[] for _ in chains]
    for e in heads:
        d, st_ref, dec_ref = e["d"], e["st"], e["dec"]
        s_t = e["s_ref"][d, e["h"]]
        o = [None] * n_sub
        for c in (range(n_sub) if e["fwd"] else range(n_sub - 1, -1, -1)):
            sl = slice(c * CHUNK, (c + 1) * CHUNK)
            o[c] = e["oi"][c] + _dot_nt(e["sel"](st_ref[2, d, sl, :]), s_t.astype(BF16))
            s_t = dec_ref[d, c * 8:c * 8 + 1, :] * s_t + e["ds"][c]
        e["s_ref"][d, e["h"]] = s_t
        outs[e["ci"]].append(jnp.concatenate(o, axis=0))
    return outs


def _pipelined_steps(n_blocks, stage_fn, mm_fn, slots):
    n_pairs = n_blocks // 2

    def pair(p, fin0, fin1, last):
        stage_fn(2 * p + 1, slots[1])
        mm_fn(2 * p, slots[0], fin0)
        if not last:
            stage_fn(2 * p + 2, slots[0])
        mm_fn(2 * p + 1, slots[1], fin1)

    groups = []
    for p in range(n_pairs):
        key = (2 * p >= n_pairs, 2 * p + 1 >= n_pairs, p == n_pairs - 1)
        if groups and groups[-1][2] == key:
            groups[-1][1] = p + 1
        else:
            groups.append([p, p + 1, key])

    stage_fn(0, slots[0])
    for lo, hi, key in groups:
        if hi - lo == 1:
            pair(lo, *key)
        else:
            def body(p, carry, key=key):
                pair(p, *key)
                return carry
            lax.fori_loop(lo, hi, body, 0)


def _block_rows(step, d, n_blocks, blk_rows):
    blk = step if d == 0 else n_blocks - 1 - step
    if isinstance(blk, int):
        return pl.ds(blk * blk_rows, blk_rows)
    return pl.ds(pl.multiple_of(blk * blk_rows, blk_rows), blk_rows)


def _emit_block(outs, rows, col0, fin, acc_ref, g_ref, nw, o_ref):
    for h, o in enumerate(outs):
        cols = slice(col0 + h * o.shape[1], col0 + (h + 1) * o.shape[1])
        if fin:
            g = g_ref[rows, cols].astype(F32)
            o_ref[rows, cols] = (_rms(o + acc_ref[rows, cols], nw) * (g * _sigmoid(g))).astype(o_ref.dtype)
        else:
            acc_ref[rows, cols] = o


def _mixer_kernel(heads_per_stream, q_ref, kf_ref, kb_ref, v_ref, g_ref, hlf_ref, hlb_ref, nw_ref, mask_ref,
                  o_ref, acc_ref, s_ref, st_a, dec_a, st_b, dec_b):
    blk_rows = mask_ref.shape[1]
    n_blocks = q_ref.shape[0] // blk_rows
    n_streams = q_ref.shape[1] // LANES
    dv = v_ref.shape[1] // (n_streams * heads_per_stream)
    nw = nw_ref[...]
    k_refs = (kf_ref, kb_ref)
    hl_refs = (hlf_ref, hlb_ref)
    if heads_per_stream == 1:
        q_masks = [None]
    else:
        lane = lax.broadcasted_iota(jnp.int32, (1, LANES), 1)
        q_masks = [lane < GLA_DK, lane >= GLA_DK]

    def stage_fn(step, slot):
        chains = []
        for d in range(2):
            tri = mask_ref[d].astype(BF16)
            rows = _block_rows(step, d, n_blocks, blk_rows)
            for s in range(n_streams):
                cs = slice(s * LANES, (s + 1) * LANES)
                chains.append((lambda rows=rows, cs=cs: q_ref[rows, cs].astype(F32),
                               lambda rows=rows, cs=cs, d=d: k_refs[d][rows, cs].astype(F32),
                               hl_refs[d][rows, 2 * s * LANES:2 * (s + 1) * LANES], tri, d == 0,
                               slot[0].at[s], slot[1].at[s], d))
        _stage_chains(chains)

    def mm_fn(step, slot, fin):
        chains, where = [], []
        for d in range(2):
            rows = _block_rows(step, d, n_blocks, blk_rows)
            for s in range(n_streams):
                col0 = s * heads_per_stream * dv
                v_heads = [v_ref[rows, col0 + h * dv:col0 + (h + 1) * dv] for h in range(heads_per_stream)]
                chains.append((slot[0].at[s], slot[1].at[s], d, v_heads, q_masks, mask_ref[d],
                               s_ref.at[s], d == 0))
                where.append((rows, col0))
        for outs, (rows, col0) in zip(_mm_chains(chains), where):
            _emit_block(outs, rows, col0, fin, acc_ref, g_ref, nw, o_ref)

    s_ref[...] = jnp.zeros_like(s_ref)
    _pipelined_steps(n_blocks, stage_fn, mm_fn, ((st_a, dec_a), (st_b, dec_b)))


def _mixer_call(name, arrays, col_units, heads_per_stream, n_streams, n_groups, dv, out_width, nw, seq_len):
    n = arrays[0].shape[0]
    br = _mixer_block_rows(seq_len)
    vw = n_streams * heads_per_stream * dv
    widths = (n_streams * LANES,) * 3 + (vw, vw) + (2 * n_streams * LANES,) * 2
    in_specs = [pl.BlockSpec((seq_len, w), lambda b, j, off=off: (b, off + j))
                for w, off in zip(widths, col_units)]
    in_specs += [pl.BlockSpec((1, dv), lambda b, j: (0, 0)), pl.BlockSpec((2, br, br), lambda b, j: (0, 0, 0))]
    slot = [pltpu.VMEM((n_streams, 4, 2, br, LANES), BF16),
            pltpu.VMEM((n_streams, 2, br // CHUNK * 8, LANES), F32)]
    return pl.pallas_call(
        functools.partial(_mixer_kernel, heads_per_stream),
        grid=(n // seq_len, n_groups),
        in_specs=in_specs,
        out_specs=pl.BlockSpec((seq_len, vw), lambda b, j: (b, j)),
        out_shape=jax.ShapeDtypeStruct((n, out_width), BF16),
        scratch_shapes=[pltpu.VMEM((seq_len, vw), F32),
                        pltpu.VMEM((n_streams, 2, heads_per_stream, dv, LANES), F32)] + slot + slot,
        compiler_params=pltpu.CompilerParams(dimension_semantics=("arbitrary", "arbitrary"),
                                             vmem_limit_bytes=MIXER_VMEM_LIMIT),
        name=name,
    )(*arrays, nw, _block_masks(br))


def _mixer_block_rows(seq_len):
    return min(MIXER_BLOCK_ROWS, seq_len // 2)


def _mixer_streams(seq_len, bytes_per_row_per_stream, max_streams):
    return next(ns for ns in (4, 2, 1)
                if ns <= max_streams and (ns == 1 or ns * seq_len * bytes_per_row_per_stream <= MIXER_VMEM_BUDGET))


def _hg_mixer(pa, pk, plog, nw, seq_len):
    ns = _mixer_streams(seq_len, (5 * 2 + 2 * 4 + 2) * 2 * LANES + 4 * LANES, HG_HEADS)
    ng = HG_HEADS // ns
    units = (0, 0, HG_HEADS // ns, HG_HEADS // ns, 2 * HG_HEADS // ns, 0, HG_HEADS // ns)
    return _mixer_call("hg_mixer", (pa, pk, pk, pa, pa, plog, plog), units, 1, ns, ng, HG_DK, HG_WIDTH,
                       nw, seq_len)


def _gla_mixer(pa, plog, nw, seq_len):
    npair = GLA_HEADS // 2
    ns = _mixer_streams(seq_len, (3 * 2 + 2 * 4 + 2 * 4 + 4) * 2 * LANES + 8 * LANES, npair)
    ng = npair // ns
    units = (12 // ns, 14 // ns, 14 // ns, 8 // ns, 10 // ns, 2 * HG_HEADS // ns, (2 * HG_HEADS + npair) // ns)
    return _mixer_call("gla_mixer", (pa, pa, pa, pa, pa, plog, plog), units, 2, ns, ng, GLA_DV, GLA_WIDTH,
                       nw, seq_len)


def _out_proj_body(x_ref, mh_ref, mg_ref, mod_ref, w_ref, nw_ref):
    m = mod_ref[0]
    y = _dot(mh_ref[...], w_ref[0:HG_WIDTH, :]) + _dot(mg_ref[...], w_ref[HG_WIDTH:, :])
    x = x_ref[...] + m[2:3, :] * y
    h = _rms(x, nw_ref[...]) * (1.0 + m[4:5, :]) + m[3:4, :]
    return x, h


def _out_proj_kernel(x_ref, mh_ref, mg_ref, mod_ref, w_ref, nw_ref, xo_ref, h_ref):
    x, h = _out_proj_body(x_ref, mh_ref, mg_ref, mod_ref, w_ref, nw_ref)
    xo_ref[...] = x
    h_ref[...] = h.astype(BF16)


def _out_proj_route_kernel(x_ref, mh_ref, mg_ref, mod_ref, w_ref, nw_ref, wr_ref, br_ref,
                           xo_ref, h_ref, route_ref):
    x, h = _out_proj_body(x_ref, mh_ref, mg_ref, mod_ref, w_ref, nw_ref)
    xo_ref[...] = x
    h_ref[...] = h
    wr_hi, wr_lo = _split_bf16(wr_ref[...])
    logits = _dot3(h, wr_hi, wr_lo) + br_ref[...]
    lane = lax.broadcasted_iota(jnp.int32, logits.shape, 1).astype(F32)
    neg = jnp.float32(-jnp.inf)
    logits = jnp.where(lane < N_EXPERTS, logits, neg)
    m1 = jnp.max(logits, axis=-1, keepdims=True)
    i1 = jnp.min(jnp.where(logits == m1, lane, float(LANES)), axis=-1, keepdims=True)
    rest = jnp.where(lane == i1, neg, logits)
    m2 = jnp.max(rest, axis=-1, keepdims=True)
    i2 = jnp.min(jnp.where(rest == m2, lane, float(LANES)), axis=-1, keepdims=True)
    e2 = jnp.exp(m2 - m1)
    den = 1.0 / (1.0 + e2)
    route = jnp.where(lane == 0.0, i1,
                      jnp.where(lane == 1.0, i2,
                                jnp.where(lane == 2.0, den, jnp.where(lane == 3.0, e2 * den, 0.0))))
    route_ref[...] = route


def _out_proj(x, mh, mg, mod, w, nw, seq_len, tm, router=None):
    n, d = x.shape
    tps = seq_len // tm
    in_specs = [
        pl.BlockSpec((tm, d), lambda i: (i, 0)),
        pl.BlockSpec((tm, HG_WIDTH), lambda i: (i, 0)),
        pl.BlockSpec((tm, GLA_WIDTH), lambda i: (i, 0)),
        pl.BlockSpec((1, 6, d), lambda i: (i // tps, 0, 0)),
        pl.BlockSpec((d, d), lambda i: (0, 0)),
        pl.BlockSpec((1, d), lambda i: (0, 0)),
    ]
    out_specs = [pl.BlockSpec((tm, d), lambda i: (i, 0)), pl.BlockSpec((tm, d), lambda i: (i, 0))]
    out_shape = [jax.ShapeDtypeStruct((n, d), F32), jax.ShapeDtypeStruct((n, d), BF16)]
    args = [x, mh, mg, mod, w, nw]
    body = _out_proj_kernel
    if router is not None:
        in_specs += [pl.BlockSpec((d, LANES), lambda i: (0, 0)), pl.BlockSpec((1, LANES), lambda i: (0, 0))]
        out_specs.append(pl.BlockSpec((tm, LANES), lambda i: (i, 0)))
        out_shape[1] = jax.ShapeDtypeStruct((n, d), F32)
        out_shape.append(jax.ShapeDtypeStruct((n, LANES), F32))
        args += list(router)
        body = _out_proj_route_kernel
    return pl.pallas_call(
        body,
        grid=(n // tm,),
        in_specs=in_specs,
        out_specs=out_specs,
        out_shape=out_shape,
        compiler_params=_cparams("arbitrary"),
        name="out_proj",
    )(*args)


def _swiglu_step(xb, w1_ref, w3_ref, w2_ref, acc_ref):
    f = pl.program_id(1)
    a = _dot(xb, w1_ref[0])
    b = _dot(xb, w3_ref[0])
    part = _dot((a * _sigmoid(a) * b).astype(BF16), w2_ref[0])

    @pl.when(f == 0)
    def _():
        acc_ref[...] = part

    @pl.when(f > 0)
    def _():
        acc_ref[...] += part


def _dense_ffn_kernel(h_ref, x_ref, mod_ref, w1_ref, w3_ref, w2_ref, o_ref, acc_ref):
    _swiglu_step(h_ref[...], w1_ref, w3_ref, w2_ref, acc_ref)

    @pl.when(pl.program_id(1) == pl.num_programs(1) - 1)
    def _():
        o_ref[...] = x_ref[...] + mod_ref[0][5:6, :] * acc_ref[...]


def _dense_ffn(h, x, mod, w1, w3, w2, seq_len, tm, tf):
    n, d = x.shape
    ff = w1.shape[-1]
    tps = seq_len // tm
    return pl.pallas_call(
        _dense_ffn_kernel,
        grid=(n // tm, ff // tf),
        in_specs=[
            pl.BlockSpec((tm, d), lambda i, f: (i, 0)),
            pl.BlockSpec((tm, d), lambda i, f: (i, 0)),
            pl.BlockSpec((1, 6, d), lambda i, f: (i // tps, 0, 0)),
            pl.BlockSpec((1, d, tf), lambda i, f: (0, 0, f)),
            pl.BlockSpec((1, d, tf), lambda i, f: (0, 0, f)),
            pl.BlockSpec((1, tf, d), lambda i, f: (0, f, 0)),
        ],
        out_specs=pl.BlockSpec((tm, d), lambda i, f: (i, 0)),
        out_shape=jax.ShapeDtypeStruct((n, d), F32),
        scratch_shapes=[pltpu.VMEM((tm, d), F32)],
        compiler_params=_cparams("arbitrary", "arbitrary"),
        name="dense_ffn",
    )(h, x, mod, w1, w3, w2)


def _expert_ffn_kernel(be_ref, nu_ref, nv_ref, xs_ref, w1_ref, w3_ref, w2_ref, o_ref, acc_ref):
    i = pl.program_id(0)
    last = pl.program_id(1) == pl.num_programs(1) - 1
    used = i < nu_ref[0]

    @pl.when(used)
    def _():
        row = lax.broadcasted_iota(jnp.int32, (xs_ref.shape[0], 1), 0)
        xb = jnp.where(row < nv_ref[i], xs_ref[...], 0.0).astype(BF16)
        _swiglu_step(xb, w1_ref, w3_ref, w2_ref, acc_ref)

    @pl.when(jnp.logical_and(used, last))
    def _():
        o_ref[...] = acc_ref[...].astype(o_ref.dtype)

    @pl.when(jnp.logical_and(jnp.logical_not(used), last))
    def _():
        o_ref[...] = jnp.zeros_like(o_ref)


def _expert_ffn(xs, block_e, n_used, n_valid, w1, w3, w2, tm, tf):
    rows, d = xs.shape
    ff = w1.shape[-1]
    nf = ff // tf

    def row_idx(i, f, be, nu, nv):
        return (jnp.minimum(i, nu[0] - 1), 0)

    def fsel(i, f, nu):
        return jnp.where(i < nu[0], f, nf - 1)

    wmode = dict(pipeline_mode=pl.Buffered(1)) if nf == 1 else {}
    grid_spec = pltpu.PrefetchScalarGridSpec(
        num_scalar_prefetch=3,
        grid=(rows // tm, nf),
        in_specs=[
            pl.BlockSpec((tm, d), row_idx),
            pl.BlockSpec((1, d, tf), lambda i, f, be, nu, nv: (be[i], 0, fsel(i, f, nu)), **wmode),
            pl.BlockSpec((1, d, tf), lambda i, f, be, nu, nv: (be[i], 0, fsel(i, f, nu)), **wmode),
            pl.BlockSpec((1, tf, d), lambda i, f, be, nu, nv: (be[i], fsel(i, f, nu), 0), **wmode),
        ],
        out_specs=pl.BlockSpec((tm, d), lambda i, f, be, nu, nv: (i, 0)),
        scratch_shapes=[pltpu.VMEM((tm, d), F32)],
    )
    return pl.pallas_call(
        _expert_ffn_kernel,
        grid_spec=grid_spec,
        out_shape=jax.ShapeDtypeStruct((rows, d), F32),
        compiler_params=_cparams("arbitrary", "arbitrary"),
        name="expert_ffn",
    )(block_e, n_used, n_valid, xs, w1, w3, w2)


def _scatter_rows(src, dest_a, dest_b, n_out):
    n, d = src.shape
    n_workers = SC_CORES * SC_SUBCORES
    per_worker = n // n_workers
    chunk = next(c for c in (32, 16, 8) if per_worker % c == 0)
    n_chunks = per_worker // chunk
    mesh = plsc.VectorSubcoreMesh(core_axis_name="c", subcore_axis_name="s")

    @functools.partial(
        pl.kernel, mesh=mesh, out_type=jax.ShapeDtypeStruct((n_out, d), src.dtype),
        scratch_types=[pltpu.VMEM((chunk,), jnp.int32), pltpu.VMEM((chunk,), jnp.int32),
                       pltpu.VMEM((chunk, d), src.dtype)])
    def scatter_kernel(src_hbm, da_hbm, db_hbm, out_hbm, ia_v, ib_v, rows_v):
        worker = lax.axis_index("s") * SC_CORES + lax.axis_index("c")
        base = worker * per_worker

        @pl.loop(0, n_chunks)
        def _(j):
            off = pl.multiple_of(base + j * chunk, chunk)
            pltpu.sync_copy(da_hbm.at[pl.ds(off, chunk)], ia_v)
            pltpu.sync_copy(db_hbm.at[pl.ds(off, chunk)], ib_v)
            pltpu.sync_copy(src_hbm.at[pl.ds(off, chunk)], rows_v)
            pltpu.sync_copy(rows_v, out_hbm.at[ia_v])
            pltpu.sync_copy(rows_v, out_hbm.at[ib_v])

    return scatter_kernel(src, dest_a, dest_b)


def _gather_rows(table, idx):
    n_rows = idx.shape[0]
    d = table.shape[1]
    n_workers = SC_CORES * SC_SUBCORES
    per_worker = n_rows // n_workers
    chunk = next(c for c in (64, 32, 16, 8) if per_worker % c == 0)
    n_chunks = per_worker // chunk
    mesh = plsc.VectorSubcoreMesh(core_axis_name="c", subcore_axis_name="s")

    @functools.partial(
        pl.kernel, mesh=mesh, out_type=jax.ShapeDtypeStruct((n_rows, d), table.dtype),
        scratch_types=[pltpu.VMEM((chunk,), jnp.int32), pltpu.VMEM((chunk, d), table.dtype),
                       pltpu.SemaphoreType.DMA])
    def gather_kernel(table_hbm, idx_hbm, out_hbm, idx_v, rows_v, sem):
        worker = lax.axis_index("s") * SC_CORES + lax.axis_index("c")
        base = worker * per_worker

        @pl.loop(0, n_chunks)
        def _(j):
            off = pl.multiple_of(base + j * chunk, chunk)
            pltpu.sync_copy(idx_hbm.at[pl.ds(off, chunk)], idx_v)
            pltpu.async_copy(table_hbm.at[idx_v], rows_v, sem).wait()
            pltpu.sync_copy(rows_v, out_hbm.at[pl.ds(off, chunk)])

    return gather_kernel(table, idx)


def _combine_body(x_ref, ya_ref, yb_ref, route_ref, mod_ref):
    r = route_ref[...]
    f = r[:, TOP_K:TOP_K + 1] * ya_ref[...] + r[:, TOP_K + 1:TOP_K + 2] * yb_ref[...]
    return x_ref[...] + mod_ref[0][5:6, :] * f


def _combine_kernel(x_ref, ya_ref, yb_ref, route_ref, mod_ref, o_ref):
    o_ref[...] = _combine_body(x_ref, ya_ref, yb_ref, route_ref, mod_ref)


def _combine_norm_kernel(x_ref, ya_ref, yb_ref, route_ref, mod_ref, nw_ref, o_ref):
    o_ref[...] = _rms(_combine_body(x_ref, ya_ref, yb_ref, route_ref, mod_ref), nw_ref[...])


def _combine(x, y2, route, mod, seq_len, tm, final_nw=None):
    n, d = x.shape
    tps = seq_len // tm
    nt = n // tm
    in_specs = [
        pl.BlockSpec((tm, d), lambda i: (i, 0)),
        pl.BlockSpec((tm, d), lambda i: (i, 0)),
        pl.BlockSpec((tm, d), lambda i: (i + nt, 0)),
        pl.BlockSpec((tm, LANES), lambda i: (i, 0)),
        pl.BlockSpec((1, 6, d), lambda i: (i // tps, 0, 0)),
    ]
    args = [x, y2, y2, route, mod]
    body = _combine_kernel
    if final_nw is not None:
        in_specs.append(pl.BlockSpec((1, d), lambda i: (0, 0)))
        args.append(final_nw)
        body = _combine_norm_kernel
    return pl.pallas_call(
        body,
        grid=(n // tm,),
        in_specs=in_specs,
        out_specs=pl.BlockSpec((tm, d), lambda i: (i, 0)),
        out_shape=jax.ShapeDtypeStruct((n, d), F32),
        compiler_params=_cparams("arbitrary"),
        name="moe_combine",
    )(*args)


def _final_norm_kernel(x_ref, w_ref, o_ref):
    o_ref[...] = _rms(x_ref[...], w_ref[...])


def _final_norm(x, w, tm):
    n, d = x.shape
    return pl.pallas_call(
        _final_norm_kernel,
        grid=(n // tm,),
        in_specs=[pl.BlockSpec((tm, d), lambda i: (i, 0)), pl.BlockSpec((1, d), lambda i: (0, 0))],
        out_specs=pl.BlockSpec((tm, d), lambda i: (i, 0)),
        out_shape=jax.ShapeDtypeStruct((n, d), F32),
        compiler_params=_cparams("arbitrary"),
        name="final_norm",
    )(x, w)


def _moe(h, x, mod, route, w1, w3, w2, seq_len, tm, tf, tm_c, final_nw):
    n, d = x.shape
    n_asg = n * TOP_K
    flat_e = route[:, :TOP_K].astype(jnp.int32).reshape(-1)
    onehot = (flat_e[:, None] == jnp.arange(N_EXPERTS, dtype=jnp.int32)[None, :]).astype(jnp.int32)
    csum = jnp.cumsum(onehot, axis=0)
    counts = csum[-1]
    rank = jnp.take_along_axis(csum, flat_e[:, None], axis=1)[:, 0] - 1
    pcounts = (counts + tm - 1) // tm * tm
    pend = jnp.cumsum(pcounts)
    pstart = pend - pcounts
    dest = pstart[flat_e] + rank
    n_blocks = -(-n_asg // tm) + N_EXPERTS
    block_start = jnp.arange(n_blocks, dtype=pend.dtype) * tm
    block_e = jnp.minimum(jnp.sum((pend[None, :] <= block_start[:, None]).astype(jnp.int32), axis=1),
                          N_EXPERTS - 1)
    n_used = (pend[-1:] // tm).astype(jnp.int32)
    n_valid = jnp.clip((pstart + counts)[block_e] - block_start, 0, tm).astype(jnp.int32)
    block_e = jnp.where(jnp.arange(n_blocks) < n_used[0], block_e, block_e[jnp.maximum(n_used[0] - 1, 0)])
    dest_ab = dest.reshape(n, TOP_K).T
    xs = _scatter_rows(h, dest_ab[0], dest_ab[1], n_blocks * tm)
    ys = _expert_ffn(xs, block_e, n_used, n_valid, w1, w3, w2, tm, tf)
    y2 = _gather_rows(ys, dest_ab.reshape(-1))
    return _combine(x, y2, route, mod, seq_len, tm_c, final_nw)


def _prep_weights(w_in, gla_w_gk_up, gla_b_gk, hg_lb_logits):
    depth = w_in.shape[0]
    sizes = (HG_WIDTH,) * 5 + (GLA_QK, GLA_QK, GLA_WIDTH, GLA_WIDTH, GLA_GATE_RANK, GLA_GATE_RANK)
    offs = [0]
    for s in sizes:
        offs.append(offs[-1] + s)
    seg = lambda k: w_in[:, :, offs[k]:offs[k + 1]]
    hq, hf_f, hf_b, hi, hg, gq, gk, gv, gg, lr_f, lr_b = [seg(k) for k in range(11)]
    pad = jnp.zeros(w_in.shape[:2] + (LANES - 2 * GLA_GATE_RANK,), w_in.dtype)
    w_perm = jnp.concatenate([hq, hi, hg, gq, gk, gv, gg, hf_f, hf_b, lr_f, lr_b, pad], axis=-1).astype(BF16)

    r = GLA_GATE_RANK
    wup = jnp.zeros((depth, LANES, GLA_HEADS // 2, 2, LANES), F32)
    bup = jnp.zeros((depth, GLA_HEADS // 2, 2, LANES), F32)
    for dd in range(2):
        wup = wup.at[:, dd * r:(dd + 1) * r, :, dd, :].set(
            gla_w_gk_up[:, dd].reshape(depth, r, GLA_HEADS // 2, LANES))
        bup = bup.at[:, :, dd, :].set(gla_b_gk[:, dd].reshape(depth, GLA_HEADS // 2, LANES))
    wup = wup.reshape(depth, LANES, GLA_HEADS * LANES).astype(BF16)
    bup = bup.reshape(depth, 1, GLA_HEADS * LANES)

    lb = jnp.cumsum(jax.nn.softmax(hg_lb_logits.astype(F32), axis=0), axis=0)
    lb = lb - lb[0]
    return w_perm, wup, bup, lb


def _pick(pref, total):
    t = min(pref, total)
    while total % t:
        t //= 2
    return t


def _trunk(x3, mods, weights):
    (norm1_w, w_perm, lb, wup, bup, hg_norm_w, gla_norm_w, w_out, norm2_w, w_ff1, w_ff3, w_ff2,
     w_router, b_router, w_e1, w_e3, w_e2, final_norm_w) = weights
    nb, seq_len, d = x3.shape
    depth = norm1_w.shape[0]
    n = nb * seq_len
    x = x3.reshape(n, d)
    tm = _pick(512, seq_len)
    tm_ffn = _pick(512, seq_len)
    tm_moe = _pick(512, n * TOP_K)
    for l in range(depth):
        mod = mods[l]
        pa, pk, plog = _in_proj(x, mod, norm1_w[l][None, :], w_perm[l], lb[l], wup[l], bup[l], seq_len, tm)
        mh = _hg_mixer(pa, pk, plog, hg_norm_w[l][None, :], seq_len)
        mg = _gla_mixer(pa, plog, gla_norm_w[l][None, :], seq_len)
        m = l // 2
        if l % 2 == 0:
            x, h = _out_proj(x, mh, mg, mod, w_out[l], norm2_w[l][None, :], seq_len, tm)
            x = _dense_ffn(h, x, mod, w_ff1[m:m + 1], w_ff3[m:m + 1], w_ff2[m:m + 1], seq_len, tm_ffn,
                           w_ff1.shape[-1])
        else:
            wr = jnp.pad(w_router[m], ((0, 0), (0, LANES - N_EXPERTS)))
            br = jnp.pad(b_router[m], (0, LANES - N_EXPERTS))[None, :]
            x, h, route = _out_proj(x, mh, mg, mod, w_out[l], norm2_w[l][None, :], seq_len, tm,
                                    router=(wr, br))
            final_nw = final_norm_w[None, :] if l == depth - 1 else None
            x = _moe(h, x, mod, route, w_e1[m], w_e3[m], w_e2[m], seq_len, tm_moe,
                     w_e1.shape[-1], tm, final_nw)
    if depth % 2 == 1:
        x = _final_norm(x, final_norm_w[None, :], tm)
    return x.reshape(nb, seq_len, d)


def kernel(x_prompt, x_sample, c_prompt, c_sample, w_ada, b_ada, norm1_w, w_in, hg_lb_logits, gla_w_gk_up, gla_b_gk, hg_norm_w, gla_norm_w, w_out, norm2_w, w_ff1, w_ff3, w_ff2, w_router, b_router, w_e1, w_e3, w_e2, final_norm_w):
    depth, d = norm1_w.shape
    w_perm, wup, bup, lb = _prep_weights(w_in, gla_w_gk_up, gla_b_gk, hg_lb_logits)
    nbp = c_prompt.shape[0]
    mods = _ada_mods(jnp.concatenate([c_prompt, c_sample], axis=0), w_ada, b_ada)
    mods = mods.reshape(depth, mods.shape[1], 6, d)
    weights = (norm1_w, w_perm, lb, wup, bup, hg_norm_w, gla_norm_w, w_out.astype(BF16), norm2_w,
               w_ff1.astype(BF16), w_ff3.astype(BF16), w_ff2.astype(BF16), w_router, b_router,
               w_e1.astype(BF16), w_e3.astype(BF16), w_e2.astype(BF16), final_norm_w)
    y_prompt = _trunk(x_prompt, mods[:, :nbp], weights)
    y_sample = _trunk(x_sample, mods[:, nbp:], weights)
    return (y_prompt, y_sample)
```

```python
import functools

import jax
import jax.numpy as jnp
from jax import lax
from jax.experimental import pallas as pl
from jax.experimental.pallas import tpu as pltpu
from jax.experimental.pallas import tpu_sc as plsc

F32 = jnp.float32
BF16 = jnp.bfloat16

D_MODEL = 1024
HG_HEADS = 4
HG_DK = 128
HG_WIDTH = 512
GLA_HEADS = 4
GLA_DK = 64
GLA_DV = 128
GLA_QK = 256
GLA_WIDTH = 512
GLA_GATE_RANK = 16
GLA_GATE_NORMALIZER = 16.0
CHUNK = 64
N_EXPERTS = 8
TOP_K = 2
EPS = 1e-6
LANES = 128

PA_WIDTH = 3 * HG_WIDTH + 2 * GLA_QK + 2 * GLA_WIDTH
PG_WIDTH = 2 * HG_WIDTH + LANES
PK_WIDTH = 2 * HG_WIDTH
PL_WIDTH = (2 * HG_HEADS + 2 * (GLA_HEADS // 2)) * 2 * LANES

VMEM_LIMIT = 48 * 1024 * 1024
SC_CORES = 2
SC_SUBCORES = 16
MIXER_BLOCK_ROWS = 128
MIXER_VMEM_LIMIT = 56 * 1024 * 1024
MIXER_VMEM_BUDGET = 46 * 1024 * 1024


def _cparams(*sem):
    return pltpu.CompilerParams(dimension_semantics=sem, vmem_limit_bytes=VMEM_LIMIT)


def _sigmoid(x):
    return 1.0 / (1.0 + jnp.exp(-x))


def _split_bf16(a):
    hi = a.astype(BF16)
    lo = (a - hi.astype(F32)).astype(BF16)
    return hi, lo


def _dot(a, b):
    return jnp.dot(a, b, preferred_element_type=F32)


def _dot_nt(a, b):
    return lax.dot_general(a, b, (((1,), (1,)), ((), ())), preferred_element_type=F32)


def _dot_tn(a, b):
    return lax.dot_general(a, b, (((0,), (0,)), ((), ())), preferred_element_type=F32)


def _dot3(a, b_hi, b_lo):
    a_hi, a_lo = _split_bf16(a)
    return _dot(a_hi, b_hi) + (_dot(a_hi, b_lo) + _dot(a_lo, b_hi))


def _rms(x, w):
    ms = jnp.mean(x * x, axis=-1, keepdims=True)
    return x * lax.rsqrt(ms + EPS) * w


def _ada_kernel(c_ref, w_ref, b_ref, o_ref):
    c = c_ref[...]
    s = c * _sigmoid(c)
    w_hi, w_lo = _split_bf16(w_ref[0])
    o_ref[0] = _dot3(s, w_hi, w_lo) + b_ref[0]


def _ada_mods(c, w_ada, b_ada):
    depth, d, e = w_ada.shape
    nb = c.shape[0]
    tn = 1536
    return pl.pallas_call(
        _ada_kernel,
        grid=(depth, e // tn),
        in_specs=[
            pl.BlockSpec((nb, d), lambda l, j: (0, 0)),
            pl.BlockSpec((1, d, tn), lambda l, j: (l, 0, j)),
            pl.BlockSpec((1, 1, tn), lambda l, j: (l, 0, j)),
        ],
        out_specs=pl.BlockSpec((1, nb, tn), lambda l, j: (l, 0, j)),
        out_shape=jax.ShapeDtypeStruct((depth, nb, e), F32),
        compiler_params=_cparams("arbitrary", "arbitrary"),
        name="ada_mods",
    )(c, w_ada, b_ada.reshape(depth, 1, e))


def _hg_gate(z, lb):
    e = jnp.exp(-jnp.abs(z))
    r = 1.0 / (1.0 + e)
    er = e * r
    pos = z >= 0.0
    sig = jnp.where(pos, r, er)
    sig_neg = jnp.where(pos, er, r)
    oml = 1.0 - lb
    return oml * sig_neg, jnp.log(lb + oml * sig)


def _store_slabs(ref, slab0, val):
    for j in range(val.shape[1] // LANES):
        ref[slab0 + j] = val[:, j * LANES:(j + 1) * LANES].astype(ref.dtype)


def _store_hi_lo(pl_ref, blk, logg):
    hi, lo = _split_bf16(logg)
    pl_ref[2 * blk] = hi
    pl_ref[2 * blk + 1] = lo


def _in_proj_kernel(x_ref, mod_ref, nw_ref, w_ref, lb_ref, wup_ref, bup_ref, pa_ref, pk_ref, pl_ref):
    m = mod_ref[0]
    h = _rms(x_ref[...], nw_ref[...]) * (1.0 + m[1:2, :]) + m[0:1, :]
    hb = h.astype(BF16)
    proj = lambda lo, hi: _dot(hb, w_ref[:, lo:hi])

    def plain(c):
        _store_slabs(pa_ref, 4 * c, proj(c * 512, (c + 1) * 512))

    lb = lb_ref[...]
    for d in range(2):
        z = proj(PA_WIDTH + d * HG_WIDTH, PA_WIDTH + (d + 1) * HG_WIDTH)
        k, logf = _hg_gate(z, lb[d:d + 1, :])
        _store_slabs(pk_ref, d * HG_HEADS, k)
        for hh in range(HG_HEADS):
            _store_hi_lo(pl_ref, d * HG_HEADS + hh, logf[:, hh * HG_DK:(hh + 1) * HG_DK])
        plain(1 + d)
    lr = proj(PA_WIDTH + 2 * HG_WIDTH, PA_WIDTH + PG_WIDTH).astype(BF16)
    zg = _dot(lr, wup_ref[...]) + bup_ref[...]
    logg = (jnp.minimum(zg, 0.0) - jnp.log(1.0 + jnp.exp(-jnp.abs(zg)))) * (1.0 / GLA_GATE_NORMALIZER)
    npair = GLA_HEADS // 2
    for p in range(npair):
        for d in range(2):
            src = (p * 2 + d) * LANES
            _store_hi_lo(pl_ref, 2 * HG_HEADS + d * npair + p, logg[:, src:src + LANES])
    plain(4)
    q = proj(0, HG_WIDTH)
    _store_slabs(pa_ref, 0, q * _sigmoid(q) * (HG_DK ** -0.5))
    plain(5)
    qk = proj(3 * 512, 4 * 512)
    _store_slabs(pa_ref, 12, qk[:, :GLA_QK] * (GLA_DK ** -0.5))
    _store_slabs(pa_ref, 12 + GLA_QK // LANES, qk[:, GLA_QK:])


def _in_proj(x, mod, nw, w, lb, wup, bup, seq_len, tm):
    n, d = x.shape
    tps = seq_len // tm
    const = lambda shape: pl.BlockSpec(shape, lambda i: (0, 0))
    return pl.pallas_call(
        _in_proj_kernel,
        grid=(n // tm,),
        in_specs=[
            pl.BlockSpec((tm, d), lambda i: (i, 0)),
            pl.BlockSpec((1, 6, d), lambda i: (i // tps, 0, 0)),
            const((1, d)),
            const((d, PA_WIDTH + PG_WIDTH)),
            const((2, HG_WIDTH)),
            const((LANES, GLA_HEADS * LANES)),
            const((1, GLA_HEADS * LANES)),
        ],
        out_specs=[pl.BlockSpec((w // LANES, tm, LANES), lambda i: (0, i, 0))
                   for w in (PA_WIDTH, PK_WIDTH, PL_WIDTH)],
        out_shape=[jax.ShapeDtypeStruct((w // LANES, n, LANES), BF16) for w in (PA_WIDTH, PK_WIDTH, PL_WIDTH)],
        compiler_params=_cparams("arbitrary"),
        name="in_proj",
    )(x, mod, nw, w, lb, wup, bup)


def _block_masks(rows):
    r = jnp.arange(rows, dtype=jnp.int32)[:, None]
    c = jnp.arange(rows, dtype=jnp.int32)[None, :]
    same = (r // CHUNK) == (c // CHUNK)
    return jnp.stack([same & (c <= r), same & (c >= r)]).astype(F32)


def _chunk_slices(a, n_sub):
    return [a[c * CHUNK:(c + 1) * CHUNK] for c in range(n_sub)]


def _decay_factors(cs, fwd):
    n_sub = cs.shape[0] // CHUNK
    w = cs.shape[1] // 2
    b = cs[:, :w] + cs[:, w:]
    r = CHUNK // 2 if fwd else CHUNK // 2 - 1
    last = CHUNK - 1 if fwd else 0
    out = []
    for bc in _chunk_slices(b, n_sub):
        b_ref = bc[r:r + 1, :]
        b_last = bc[last:last + 1, :]
        out.append((jnp.exp(bc - b_ref), jnp.exp(b_ref - bc), jnp.exp(b_ref), jnp.exp(b_last - b_ref),
                    jnp.exp(b_last)))
    return out


def _stage_chains(chains):
    sums = [_dot(tri, hl) for _, _, hl, tri, _, _, _, _ in chains]
    for cs, (q_fn, k_fn, _, _, fwd, st_ref, dec_ref, d) in zip(sums, chains):
        fac = _decay_factors(cs, fwd)
        q = q_fn()
        k = k_fn()
        for c in range(len(fac)):
            sl = slice(c * CHUNK, (c + 1) * CHUNK)
            a, ainv, e_ref, e_last_ref, decay = fac[c]
            qa = q[sl] * a
            ka = k[sl] * ainv
            st_ref[0, d, sl, :] = qa.astype(BF16)
            st_ref[1, d, sl, :] = ka.astype(BF16)
            st_ref[2, d, sl, :] = (qa * e_ref).astype(BF16)
            st_ref[3, d, sl, :] = (ka * e_last_ref).astype(BF16)
            dec_ref[d, c * 8:(c + 1) * 8, :] = jnp.broadcast_to(decay, (8, decay.shape[1]))


def _mm_chains(chains):
    n_sub = chains[0][0].shape[2] // CHUNK
    heads = []
    for ci, (st_ref, dec_ref, d, v_heads, q_masks, maskf, s_ref, fwd) in enumerate(chains):
        for h, v in enumerate(v_heads):
            hm = q_masks[h]
            sel = (lambda a: a) if hm is None else (lambda a, hm=hm: jnp.where(hm, a, jnp.zeros_like(a)))
            heads.append(dict(ci=ci, st=st_ref, dec=dec_ref, d=d, h=h, v=v, vs=_chunk_slices(v, n_sub),
                              sel=sel, maskf=maskf, s_ref=s_ref, fwd=fwd))
    for e in heads:
        e["sc"] = _dot_nt(e["sel"](e["st"][0, e["d"]]), e["st"][1, e["d"]])
    for e in heads:
        e["ds"] = [_dot_tn(e["vs"][c], e["st"][3, e["d"], c * CHUNK:(c + 1) * CHUNK, :]) for c in range(n_sub)]
    for e in heads:
        p = jnp.where(e["maskf"] > 0.5, e["sc"], 0.0).astype(BF16)
        e["oi"] = _chunk_slices(_dot(p, e["v"]), n_sub)
    outs = [[] for _ in chains]
    for e in heads:
        d, st_ref, dec_ref = e["d"], e["st"], e["dec"]
        s_t = e["s_ref"][d, e["h"]]
        o = [None] * n_sub
        for c in (range(n_sub) if e["fwd"] else range(n_sub - 1, -1, -1)):
            sl = slice(c * CHUNK, (c + 1) * CHUNK)
            o[c] = e["oi"][c] + _dot_nt(e["sel"](st_ref[2, d, sl, :]), s_t.astype(BF16))
            s_t = dec_ref[d, c * 8:c * 8 + 1, :] * s_t + e["ds"][c]
        e["s_ref"][d, e["h"]] = s_t
        outs[e["ci"]].append(jnp.concatenate(o, axis=0))
    return outs


def _pipelined_steps(n_blocks, stage_fn, mm_fn, slots):
    n_pairs = n_blocks // 2

    def pair(p, fin0, fin1, last):
        stage_fn(2 * p + 1, slots[1])
        mm_fn(2 * p, slots[0], fin0)
        if not last:
            stage_fn(2 * p + 2, slots[0])
        mm_fn(2 * p + 1, slots[1], fin1)

    groups = []
    for p in range(n_pairs):
        key = (2 * p >= n_pairs, 2 * p + 1 >= n_pairs, p == n_pairs - 1)
        if groups and groups[-1][2] == key:
            groups[-1][1] = p + 1
        else:
            groups.append([p, p + 1, key])

    stage_fn(0, slots[0])
    for lo, hi, key in groups:
        if hi - lo == 1:
            pair(lo, *key)
        else:
            def body(p, carry, key=key):
                pair(p, *key)
                return carry
            lax.fori_loop(lo, hi, body, 0)


def _block_rows(step, d, n_blocks, blk_rows):
    blk = step if d == 0 else n_blocks - 1 - step
    if isinstance(blk, int):
        return pl.ds(blk * blk_rows, blk_rows)
    return pl.ds(pl.multiple_of(blk * blk_rows, blk_rows), blk_rows)


def _emit_block(outs, rows, col0, fin, acc_ref, g_ref, nw, o_ref):
    for h, o in enumerate(outs):
        if fin:
            g = g_ref[col0 + h, rows, :].astype(F32)
            o_ref[col0 + h, rows, :] = (_rms(o + acc_ref[col0 + h, rows, :], nw)
                                        * (g * _sigmoid(g))).astype(o_ref.dtype)
        else:
            acc_ref[col0 + h, rows, :] = o


def _mixer_kernel(heads_per_stream, q_ref, kf_ref, kb_ref, v_ref, g_ref, hlf_ref, hlb_ref, nw_ref, mask_ref,
                  o_ref, acc_ref, s_ref, st_a, dec_a, st_b, dec_b):
    blk_rows = mask_ref.shape[1]
    n_blocks = q_ref.shape[1] // blk_rows
    n_streams = q_ref.shape[0]
    nw = nw_ref[...]
    k_refs = (kf_ref, kb_ref)
    hl_refs = (hlf_ref, hlb_ref)
    if heads_per_stream == 1:
        q_masks = [None]
    else:
        lane = lax.broadcasted_iota(jnp.int32, (1, LANES), 1)
        q_masks = [lane < GLA_DK, lane >= GLA_DK]

    def stage_fn(step, slot):
        chains = []
        for d in range(2):
            tri = mask_ref[d].astype(BF16)
            rows = _block_rows(step, d, n_blocks, blk_rows)
            for s in range(n_streams):
                hl = jnp.concatenate([hl_refs[d][2 * s, rows, :], hl_refs[d][2 * s + 1, rows, :]], axis=1)
                chains.append((lambda rows=rows, s=s: q_ref[s, rows, :].astype(F32),
                               lambda rows=rows, s=s, d=d: k_refs[d][s, rows, :].astype(F32),
                               hl, tri, d == 0, slot[0].at[s], slot[1].at[s], d))
        _stage_chains(chains)

    def mm_fn(step, slot, fin):
        chains, where = [], []
        for d in range(2):
            rows = _block_rows(step, d, n_blocks, blk_rows)
            for s in range(n_streams):
                col0 = s * heads_per_stream
                v_heads = [v_ref[col0 + h, rows, :] for h in range(heads_per_stream)]
                chains.append((slot[0].at[s], slot[1].at[s], d, v_heads, q_masks, mask_ref[d],
                               s_ref.at[s], d == 0))
                where.append((rows, col0))
        for outs, (rows, col0) in zip(_mm_chains(chains), where):
            _emit_block(outs, rows, col0, fin, acc_ref, g_ref, nw, o_ref)

    s_ref[...] = jnp.zeros_like(s_ref)
    _pipelined_steps(n_blocks, stage_fn, mm_fn, ((st_a, dec_a), (st_b, dec_b)))


def _mixer_call(name, arrays, col_units, heads_per_stream, n_streams, n_groups, dv, out_width, nw, seq_len):
    n = arrays[0].shape[1]
    br = _mixer_block_rows(seq_len)
    vs = n_streams * heads_per_stream
    slabs = (n_streams,) * 3 + (vs, vs) + (2 * n_streams,) * 2
    in_specs = [pl.BlockSpec((k, seq_len, LANES), lambda b, j, off=off: (off + j, b, 0))
                for k, off in zip(slabs, col_units)]
    in_specs += [pl.BlockSpec((1, dv), lambda b, j: (0, 0)), pl.BlockSpec((2, br, br), lambda b, j: (0, 0, 0))]
    slot = [pltpu.VMEM((n_streams, 4, 2, br, LANES), BF16),
            pltpu.VMEM((n_streams, 2, br // CHUNK * 8, LANES), F32)]
    return pl.pallas_call(
        functools.partial(_mixer_kernel, heads_per_stream),
        grid=(n // seq_len, n_groups),
        in_specs=in_specs,
        out_specs=pl.BlockSpec((vs, seq_len, dv), lambda b, j: (j, b, 0)),
        out_shape=jax.ShapeDtypeStruct((out_width // dv, n, dv), BF16),
        scratch_shapes=[pltpu.VMEM((vs, seq_len, dv), F32),
                        pltpu.VMEM((n_streams, 2, heads_per_stream, dv, LANES), F32)] + slot + slot,
        compiler_params=pltpu.CompilerParams(dimension_semantics=("arbitrary", "arbitrary"),
                                             vmem_limit_bytes=MIXER_VMEM_LIMIT),
        name=name,
    )(*arrays, nw, _block_masks(br))


def _mixer_block_rows(seq_len):
    return min(MIXER_BLOCK_ROWS, seq_len // 2)


def _mixer_streams(seq_len, bytes_per_row_per_stream, max_streams):
    return next(ns for ns in (4, 2, 1)
                if ns <= max_streams and (ns == 1 or ns * seq_len * bytes_per_row_per_stream <= MIXER_VMEM_BUDGET))


def _hg_mixer(pa, pk, plog, nw, seq_len):
    ns = _mixer_streams(seq_len, (5 * 2 + 2 * 4 + 2) * 2 * LANES + 4 * LANES, HG_HEADS)
    ng = HG_HEADS // ns
    units = (0, 0, HG_HEADS // ns, HG_HEADS // ns, 2 * HG_HEADS // ns, 0, HG_HEADS // ns)
    return _mixer_call("hg_mixer", (pa, pk, pk, pa, pa, plog, plog), units, 1, ns, ng, HG_DK, HG_WIDTH,
                       nw, seq_len)


def _gla_mixer(pa, plog, nw, seq_len):
    npair = GLA_HEADS // 2
    ns = _mixer_streams(seq_len, (3 * 2 + 2 * 4 + 2 * 4 + 4) * 2 * LANES + 8 * LANES, npair)
    ng = npair // ns
    units = (12 // ns, 14 // ns, 14 // ns, 8 // ns, 10 // ns, 2 * HG_HEADS // ns, (2 * HG_HEADS + npair) // ns)
    return _mixer_call("gla_mixer", (pa, pa, pa, pa, pa, plog, plog), units, 2, ns, ng, GLA_DV, GLA_WIDTH,
                       nw, seq_len)


def _out_proj_body(x_ref, mh_ref, mg_ref, mod_ref, w_ref, nw_ref):
    m = mod_ref[0]
    mixed = jnp.concatenate([mh_ref[j] for j in range(mh_ref.shape[0])]
                            + [mg_ref[j] for j in range(mg_ref.shape[0])], axis=1)
    y = _dot(mixed, w_ref[...])
    x = x_ref[...] + m[2:3, :] * y
    h = _rms(x, nw_ref[...]) * (1.0 + m[4:5, :]) + m[3:4, :]
    return x, h


def _out_proj_kernel(x_ref, mh_ref, mg_ref, mod_ref, w_ref, nw_ref, xo_ref, h_ref):
    x, h = _out_proj_body(x_ref, mh_ref, mg_ref, mod_ref, w_ref, nw_ref)
    xo_ref[...] = x
    h_ref[...] = h.astype(BF16)


def _out_proj_route_kernel(x_ref, mh_ref, mg_ref, mod_ref, w_ref, nw_ref, wr_ref, br_ref,
                           xo_ref, h_ref, route_ref):
    x, h = _out_proj_body(x_ref, mh_ref, mg_ref, mod_ref, w_ref, nw_ref)
    xo_ref[...] = x
    h_ref[...] = h
    wr_hi, wr_lo = _split_bf16(wr_ref[...])
    logits = _dot3(h, wr_hi, wr_lo) + br_ref[...]
    lane = lax.broadcasted_iota(jnp.int32, logits.shape, 1).astype(F32)
    neg = jnp.float32(-jnp.inf)
    logits = jnp.where(lane < N_EXPERTS, logits, neg)
    m1 = jnp.max(logits, axis=-1, keepdims=True)
    i1 = jnp.min(jnp.where(logits == m1, lane, float(LANES)), axis=-1, keepdims=True)
    rest = jnp.where(lane == i1, neg, logits)
    m2 = jnp.max(rest, axis=-1, keepdims=True)
    i2 = jnp.min(jnp.where(rest == m2, lane, float(LANES)), axis=-1, keepdims=True)
    e2 = jnp.exp(m2 - m1)
    den = 1.0 / (1.0 + e2)
    route = jnp.where(lane == 0.0, i1,
                      jnp.where(lane == 1.0, i2,
                                jnp.where(lane == 2.0, den, jnp.where(lane == 3.0, e2 * den, 0.0))))
    route_ref[...] = route


def _out_proj(x, mh, mg, mod, w, nw, seq_len, tm, router=None):
    n, d = x.shape
    tps = seq_len // tm
    in_specs = [
        pl.BlockSpec((tm, d), lambda i: (i, 0)),
        pl.BlockSpec((HG_WIDTH // LANES, tm, LANES), lambda i: (0, i, 0)),
        pl.BlockSpec((GLA_WIDTH // LANES, tm, LANES), lambda i: (0, i, 0)),
        pl.BlockSpec((1, 6, d), lambda i: (i // tps, 0, 0)),
        pl.BlockSpec((d, d), lambda i: (0, 0)),
        pl.BlockSpec((1, d), lambda i: (0, 0)),
    ]
    out_specs = [pl.BlockSpec((tm, d), lambda i: (i, 0)), pl.BlockSpec((tm, d), lambda i: (i, 0))]
    out_shape = [jax.ShapeDtypeStruct((n, d), F32), jax.ShapeDtypeStruct((n, d), BF16)]
    args = [x, mh, mg, mod, w, nw]
    body = _out_proj_kernel
    if router is not None:
        in_specs += [pl.BlockSpec((d, LANES), lambda i: (0, 0)), pl.BlockSpec((1, LANES), lambda i: (0, 0))]
        out_specs.append(pl.BlockSpec((tm, LANES), lambda i: (i, 0)))
        out_shape[1] = jax.ShapeDtypeStruct((n, d), F32)
        out_shape.append(jax.ShapeDtypeStruct((n, LANES), F32))
        args += list(router)
        body = _out_proj_route_kernel
    return pl.pallas_call(
        body,
        grid=(n // tm,),
        in_specs=in_specs,
        out_specs=out_specs,
        out_shape=out_shape,
        compiler_params=_cparams("arbitrary"),
        name="out_proj",
    )(*args)


def _swiglu_step(xb, w1_ref, w3_ref, w2_ref, acc_ref):
    f = pl.program_id(1)
    a = _dot(xb, w1_ref[0])
    b = _dot(xb, w3_ref[0])
    part = _dot((a * _sigmoid(a) * b).astype(BF16), w2_ref[0])

    @pl.when(f == 0)
    def _():
        acc_ref[...] = part

    @pl.when(f > 0)
    def _():
        acc_ref[...] += part


def _dense_ffn_kernel(h_ref, x_ref, mod_ref, w1_ref, w3_ref, w2_ref, o_ref, acc_ref):
    _swiglu_step(h_ref[...], w1_ref, w3_ref, w2_ref, acc_ref)

    @pl.when(pl.program_id(1) == pl.num_programs(1) - 1)
    def _():
        o_ref[...] = x_ref[...] + mod_ref[0][5:6, :] * acc_ref[...]


def _dense_ffn(h, x, mod, w1, w3, w2, seq_len, tm, tf):
    n, d = x.shape
    ff = w1.shape[-1]
    tps = seq_len // tm
    return pl.pallas_call(
        _dense_ffn_kernel,
        grid=(n // tm, ff // tf),
        in_specs=[
            pl.BlockSpec((tm, d), lambda i, f: (i, 0)),
            pl.BlockSpec((tm, d), lambda i, f: (i, 0)),
            pl.BlockSpec((1, 6, d), lambda i, f: (i // tps, 0, 0)),
            pl.BlockSpec((1, d, tf), lambda i, f: (0, 0, f)),
            pl.BlockSpec((1, d, tf), lambda i, f: (0, 0, f)),
            pl.BlockSpec((1, tf, d), lambda i, f: (0, f, 0)),
        ],
        out_specs=pl.BlockSpec((tm, d), lambda i, f: (i, 0)),
        out_shape=jax.ShapeDtypeStruct((n, d), F32),
        scratch_shapes=[pltpu.VMEM((tm, d), F32)],
        compiler_params=_cparams("arbitrary", "arbitrary"),
        name="dense_ffn",
    )(h, x, mod, w1, w3, w2)


def _expert_ffn_kernel(be_ref, nu_ref, nv_ref, xs_ref, w1_ref, w3_ref, w2_ref, o_ref, acc_ref):
    i = pl.program_id(0)
    last = pl.program_id(1) == pl.num_programs(1) - 1
    used = i < nu_ref[0]

    @pl.when(used)
    def _():
        row = lax.broadcasted_iota(jnp.int32, (xs_ref.shape[0], 1), 0)
        xb = jnp.where(row < nv_ref[i], xs_ref[...], 0.0).astype(BF16)
        _swiglu_step(xb, w1_ref, w3_ref, w2_ref, acc_ref)

    @pl.when(jnp.logical_and(used, last))
    def _():
        o_ref[...] = acc_ref[...].astype(o_ref.dtype)

    @pl.when(jnp.logical_and(jnp.logical_not(used), last))
    def _():
        o_ref[...] = jnp.zeros_like(o_ref)


def _expert_ffn(xs, block_e, n_used, n_valid, w1, w3, w2, tm, tf):
    rows, d = xs.shape
    ff = w1.shape[-1]
    nf = ff // tf

    def row_idx(i, f, be, nu, nv):
        return (jnp.minimum(i, nu[0] - 1), 0)

    def fsel(i, f, nu):
        return jnp.where(i < nu[0], f, nf - 1)

    wmode = dict(pipeline_mode=pl.Buffered(1)) if nf == 1 else {}
    grid_spec = pltpu.PrefetchScalarGridSpec(
        num_scalar_prefetch=3,
        grid=(rows // tm, nf),
        in_specs=[
            pl.BlockSpec((tm, d), row_idx),
            pl.BlockSpec((1, d, tf), lambda i, f, be, nu, nv: (be[i], 0, fsel(i, f, nu)), **wmode),
            pl.BlockSpec((1, d, tf), lambda i, f, be, nu, nv: (be[i], 0, fsel(i, f, nu)), **wmode),
            pl.BlockSpec((1, tf, d), lambda i, f, be, nu, nv: (be[i], fsel(i, f, nu), 0), **wmode),
        ],
        out_specs=pl.BlockSpec((tm, d), lambda i, f, be, nu, nv: (i, 0)),
        scratch_shapes=[pltpu.VMEM((tm, d), F32)],
    )
    return pl.pallas_call(
        _expert_ffn_kernel,
        grid_spec=grid_spec,
        out_shape=jax.ShapeDtypeStruct((rows, d), F32),
        compiler_params=_cparams("arbitrary", "arbitrary"),
        name="expert_ffn",
    )(block_e, n_used, n_valid, xs, w1, w3, w2)


def _scatter_rows(src, dest_a, dest_b, n_out):
    n, d = src.shape
    n_workers = SC_CORES * SC_SUBCORES
    per_worker = n // n_workers
    chunk = next(c for c in (32, 16, 8) if per_worker % c == 0)
    n_chunks = per_worker // chunk
    mesh = plsc.VectorSubcoreMesh(core_axis_name="c", subcore_axis_name="s")

    @functools.partial(
        pl.kernel, mesh=mesh, out_type=jax.ShapeDtypeStruct((n_out, d), src.dtype),
        scratch_types=[pltpu.VMEM((chunk,), jnp.int32), pltpu.VMEM((chunk,), jnp.int32),
                       pltpu.VMEM((chunk, d), src.dtype)])
    def scatter_kernel(src_hbm, da_hbm, db_hbm, out_hbm, ia_v, ib_v, rows_v):
        worker = lax.axis_index("s") * SC_CORES + lax.axis_index("c")
        base = worker * per_worker

        @pl.loop(0, n_chunks)
        def _(j):
            off = pl.multiple_of(base + j * chunk, chunk)
            pltpu.sync_copy(da_hbm.at[pl.ds(off, chunk)], ia_v)
            pltpu.sync_copy(db_hbm.at[pl.ds(off, chunk)], ib_v)
            pltpu.sync_copy(src_hbm.at[pl.ds(off, chunk)], rows_v)
            pltpu.sync_copy(rows_v, out_hbm.at[ia_v])
            pltpu.sync_copy(rows_v, out_hbm.at[ib_v])

    return scatter_kernel(src, dest_a, dest_b)


def _gather_rows(table, idx):
    n_rows = idx.shape[0]
    d = table.shape[1]
    n_workers = SC_CORES * SC_SUBCORES
    per_worker = n_rows // n_workers
    chunk = next(c for c in (64, 32, 16, 8) if per_worker % c == 0)
    n_chunks = per_worker // chunk
    mesh = plsc.VectorSubcoreMesh(core_axis_name="c", subcore_axis_name="s")

    @functools.partial(
        pl.kernel, mesh=mesh, out_type=jax.ShapeDtypeStruct((n_rows, d), table.dtype),
        scratch_types=[pltpu.VMEM((chunk,), jnp.int32), pltpu.VMEM((chunk, d), table.dtype),
                       pltpu.SemaphoreType.DMA])
    def gather_kernel(table_hbm, idx_hbm, out_hbm, idx_v, rows_v, sem):
        worker = lax.axis_index("s") * SC_CORES + lax.axis_index("c")
        base = worker * per_worker

        @pl.loop(0, n_chunks)
        def _(j):
            off = pl.multiple_of(base + j * chunk, chunk)
            pltpu.sync_copy(idx_hbm.at[pl.ds(off, chunk)], idx_v)
            pltpu.async_copy(table_hbm.at[idx_v], rows_v, sem).wait()
            pltpu.sync_copy(rows_v, out_hbm.at[pl.ds(off, chunk)])

    return gather_kernel(table, idx)


def _combine_body(x_ref, ya_ref, yb_ref, route_ref, mod_ref):
    r = route_ref[...]
    f = r[:, TOP_K:TOP_K + 1] * ya_ref[...] + r[:, TOP_K + 1:TOP_K + 2] * yb_ref[...]
    return x_ref[...] + mod_ref[0][5:6, :] * f


def _combine_kernel(x_ref, ya_ref, yb_ref, route_ref, mod_ref, o_ref):
    o_ref[...] = _combine_body(x_ref, ya_ref, yb_ref, route_ref, mod_ref)


def _combine_norm_kernel(x_ref, ya_ref, yb_ref, route_ref, mod_ref, nw_ref, o_ref):
    o_ref[...] = _rms(_combine_body(x_ref, ya_ref, yb_ref, route_ref, mod_ref), nw_ref[...])


def _combine(x, y2, route, mod, seq_len, tm, final_nw=None):
    n, d = x.shape
    tps = seq_len // tm
    nt = n // tm
    in_specs = [
        pl.BlockSpec((tm, d), lambda i: (i, 0)),
        pl.BlockSpec((tm, d), lambda i: (i, 0)),
        pl.BlockSpec((tm, d), lambda i: (i + nt, 0)),
        pl.BlockSpec((tm, LANES), lambda i: (i, 0)),
        pl.BlockSpec((1, 6, d), lambda i: (i // tps, 0, 0)),
    ]
    args = [x, y2, y2, route, mod]
    body = _combine_kernel
    if final_nw is not None:
        in_specs.append(pl.BlockSpec((1, d), lambda i: (0, 0)))
        args.append(final_nw)
        body = _combine_norm_kernel
    return pl.pallas_call(
        body,
        grid=(n // tm,),
        in_specs=in_specs,
        out_specs=pl.BlockSpec((tm, d), lambda i: (i, 0)),
        out_shape=jax.ShapeDtypeStruct((n, d), F32),
        compiler_params=_cparams("arbitrary"),
        name="moe_combine",
    )(*args)


def _final_norm_kernel(x_ref, w_ref, o_ref):
    o_ref[...] = _rms(x_ref[...], w_ref[...])


def _final_norm(x, w, tm):
    n, d = x.shape
    return pl.pallas_call(
        _final_norm_kernel,
        grid=(n // tm,),
        in_specs=[pl.BlockSpec((tm, d), lambda i: (i, 0)), pl.BlockSpec((1, d), lambda i: (0, 0))],
        out_specs=pl.BlockSpec((tm, d), lambda i: (i, 0)),
        out_shape=jax.ShapeDtypeStruct((n, d), F32),
        compiler_params=_cparams("arbitrary"),
        name="final_norm",
    )(x, w)


def _moe(h, x, mod, route, w1, w3, w2, seq_len, tm, tf, tm_c, final_nw):
    n, d = x.shape
    n_asg = n * TOP_K
    flat_e = route[:, :TOP_K].astype(jnp.int32).reshape(-1)
    onehot = (flat_e[:, None] == jnp.arange(N_EXPERTS, dtype=jnp.int32)[None, :]).astype(jnp.int32)
    csum = jnp.cumsum(onehot, axis=0)
    counts = csum[-1]
    rank = jnp.take_along_axis(csum, flat_e[:, None], axis=1)[:, 0] - 1
    pcounts = (counts + tm - 1) // tm * tm
    pend = jnp.cumsum(pcounts)
    pstart = pend - pcounts
    dest = pstart[flat_e] + rank
    n_blocks = -(-n_asg // tm) + N_EXPERTS
    block_start = jnp.arange(n_blocks, dtype=pend.dtype) * tm
    block_e = jnp.minimum(jnp.sum((pend[None, :] <= block_start[:, None]).astype(jnp.int32), axis=1),
                          N_EXPERTS - 1)
    n_used = (pend[-1:] // tm).astype(jnp.int32)
    n_valid = jnp.clip((pstart + counts)[block_e] - block_start, 0, tm).astype(jnp.int32)
    block_e = jnp.where(jnp.arange(n_blocks) < n_used[0], block_e, block_e[jnp.maximum(n_used[0] - 1, 0)])
    dest_ab = dest.reshape(n, TOP_K).T
    xs = _scatter_rows(h, dest_ab[0], dest_ab[1], n_blocks * tm)
    ys = _expert_ffn(xs, block_e, n_used, n_valid, w1, w3, w2, tm, tf)
    y2 = _gather_rows(ys, dest_ab.reshape(-1))
    return _combine(x, y2, route, mod, seq_len, tm_c, final_nw)


def _prep_weights(w_in, gla_w_gk_up, gla_b_gk, hg_lb_logits):
    depth = w_in.shape[0]
    sizes = (HG_WIDTH,) * 5 + (GLA_QK, GLA_QK, GLA_WIDTH, GLA_WIDTH, GLA_GATE_RANK, GLA_GATE_RANK)
    offs = [0]
    for s in sizes:
        offs.append(offs[-1] + s)
    seg = lambda k: w_in[:, :, offs[k]:offs[k + 1]]
    hq, hf_f, hf_b, hi, hg, gq, gk, gv, gg, lr_f, lr_b = [seg(k) for k in range(11)]
    pad = jnp.zeros(w_in.shape[:2] + (LANES - 2 * GLA_GATE_RANK,), w_in.dtype)
    w_perm = jnp.concatenate([hq, hi, hg, gq, gk, gv, gg, hf_f, hf_b, lr_f, lr_b, pad], axis=-1).astype(BF16)

    r = GLA_GATE_RANK
    wup = jnp.zeros((depth, LANES, GLA_HEADS // 2, 2, LANES), F32)
    bup = jnp.zeros((depth, GLA_HEADS // 2, 2, LANES), F32)
    for dd in range(2):
        wup = wup.at[:, dd * r:(dd + 1) * r, :, dd, :].set(
            gla_w_gk_up[:, dd].reshape(depth, r, GLA_HEADS // 2, LANES))
        bup = bup.at[:, :, dd, :].set(gla_b_gk[:, dd].reshape(depth, GLA_HEADS // 2, LANES))
    wup = wup.reshape(depth, LANES, GLA_HEADS * LANES).astype(BF16)
    bup = bup.reshape(depth, 1, GLA_HEADS * LANES)

    lb = jnp.cumsum(jax.nn.softmax(hg_lb_logits.astype(F32), axis=0), axis=0)
    lb = lb - lb[0]
    return w_perm, wup, bup, lb


def _pick(pref, total):
    t = min(pref, total)
    while total % t:
        t //= 2
    return t


def _trunk(x3, mods, weights):
    (norm1_w, w_perm, lb, wup, bup, hg_norm_w, gla_norm_w, w_out, norm2_w, w_ff1, w_ff3, w_ff2,
     w_router, b_router, w_e1, w_e3, w_e2, final_norm_w) = weights
    nb, seq_len, d = x3.shape
    depth = norm1_w.shape[0]
    n = nb * seq_len
    x = x3.reshape(n, d)
    tm = _pick(512, seq_len)
    tm_ffn = _pick(512, seq_len)
    tm_moe = _pick(512, n * TOP_K)
    for l in range(depth):
        mod = mods[l]
        pa, pk, plog = _in_proj(x, mod, norm1_w[l][None, :], w_perm[l], lb[l], wup[l], bup[l], seq_len, tm)
        mh = _hg_mixer(pa, pk, plog, hg_norm_w[l][None, :], seq_len)
        mg = _gla_mixer(pa, plog, gla_norm_w[l][None, :], seq_len)
        m = l // 2
        if l % 2 == 0:
            x, h = _out_proj(x, mh, mg, mod, w_out[l], norm2_w[l][None, :], seq_len, tm)
            x = _dense_ffn(h, x, mod, w_ff1[m:m + 1], w_ff3[m:m + 1], w_ff2[m:m + 1], seq_len, tm_ffn,
                           w_ff1.shape[-1])
        else:
            wr = jnp.pad(w_router[m], ((0, 0), (0, LANES - N_EXPERTS)))
            br = jnp.pad(b_router[m], (0, LANES - N_EXPERTS))[None, :]
            x, h, route = _out_proj(x, mh, mg, mod, w_out[l], norm2_w[l][None, :], seq_len, tm,
                                    router=(wr, br))
            final_nw = final_norm_w[None, :] if l == depth - 1 else None
            x = _moe(h, x, mod, route, w_e1[m], w_e3[m], w_e2[m], seq_len, tm_moe,
                     w_e1.shape[-1], tm, final_nw)
    if depth % 2 == 1:
        x = _final_norm(x, final_norm_w[None, :], tm)
    return x.reshape(nb, seq_len, d)


def kernel(x_prompt, x_sample, c_prompt, c_sample, w_ada, b_ada, norm1_w, w_in, hg_lb_logits, gla_w_gk_up, gla_b_gk, hg_norm_w, gla_norm_w, w_out, norm2_w, w_ff1, w_ff3, w_ff2, w_router, b_router, w_e1, w_e3, w_e2, final_norm_w):
    depth, d = norm1_w.shape
    w_perm, wup, bup, lb = _prep_weights(w_in, gla_w_gk_up, gla_b_gk, hg_lb_logits)
    nbp = c_prompt.shape[0]
    mods = _ada_mods(jnp.concatenate([c_prompt, c_sample], axis=0), w_ada, b_ada)
    mods = mods.reshape(depth, mods.shape[1], 6, d)
    weights = (norm1_w, w_perm, lb, wup, bup, hg_norm_w, gla_norm_w, w_out.astype(BF16), norm2_w,
               w_ff1.astype(BF16), w_ff3.astype(BF16), w_ff2.astype(BF16), w_router, b_router,
               w_e1.astype(BF16), w_e3.astype(BF16), w_e2.astype(BF16), final_norm_w)
    y_prompt = _trunk(x_prompt, mods[:, :nbp], weights)
    y_sample = _trunk(x_sample, mods[:, nbp:], weights)
    return (y_prompt, y_sample)
```

```python
import functools

import jax
import jax.numpy as jnp
from jax import lax
from jax.experimental import pallas as pl
from jax.experimental.pallas import tpu as pltpu
from jax.experimental.pallas import tpu_sc as plsc

F32 = jnp.float32
BF16 = jnp.bfloat16

D_MODEL = 1024
HG_HEADS = 4
HG_DK = 128
HG_WIDTH = 512
GLA_HEADS = 4
GLA_DK = 64
GLA_DV = 128
GLA_QK = 256
GLA_WIDTH = 512
GLA_GATE_RANK = 16
GLA_GATE_NORMALIZER = 16.0
CHUNK = 64
N_EXPERTS = 8
TOP_K = 2
EPS = 1e-6
LANES = 128

PA_WIDTH = 3 * HG_WIDTH + 2 * GLA_QK + 2 * GLA_WIDTH
PG_WIDTH = 2 * HG_WIDTH + LANES
PK_WIDTH = 2 * HG_WIDTH
PL_WIDTH = (2 * HG_HEADS + 2 * (GLA_HEADS // 2)) * 2 * LANES

VMEM_LIMIT = 48 * 1024 * 1024
SC_CORES = 2
SC_SUBCORES = 16
MIXER_BLOCK_ROWS = 128
MIXER_VMEM_LIMIT = 56 * 1024 * 1024
MIXER_VMEM_BUDGET = 46 * 1024 * 1024


def _cparams(*sem):
    return pltpu.CompilerParams(dimension_semantics=sem, vmem_limit_bytes=VMEM_LIMIT)


def _sigmoid(x):
    return 1.0 / (1.0 + jnp.exp(-x))


def _split_bf16(a):
    hi = a.astype(BF16)
    lo = (a - hi.astype(F32)).astype(BF16)
    return hi, lo


def _dot(a, b):
    return jnp.dot(a, b, preferred_element_type=F32)


def _dot_nt(a, b):
    return lax.dot_general(a, b, (((1,), (1,)), ((), ())), preferred_element_type=F32)


def _dot_tn(a, b):
    return lax.dot_general(a, b, (((0,), (0,)), ((), ())), preferred_element_type=F32)


def _dot3(a, b_hi, b_lo):
    a_hi, a_lo = _split_bf16(a)
    return _dot(a_hi, b_hi) + (_dot(a_hi, b_lo) + _dot(a_lo, b_hi))


def _pack_bf16_pair(v):
    w = v.shape[1] // 2
    lo = lax.bitcast_convert_type(v[:, :w].astype(BF16).astype(F32), jnp.uint32)
    hi = lax.bitcast_convert_type(v[:, w:].astype(BF16).astype(F32), jnp.uint32)
    return (lo >> 16) | hi


def _unpack_bf16_pair(u):
    lo = lax.bitcast_convert_type(u << 16, F32)
    hi = lax.bitcast_convert_type(u & jnp.uint32(0xFFFF0000), F32)
    return lo, hi


def _rms(x, w):
    ms = jnp.mean(x * x, axis=-1, keepdims=True)
    return x * lax.rsqrt(ms + EPS) * w


def _ada_kernel(c_ref, w_ref, b_ref, o_ref):
    c = c_ref[...]
    s = c * _sigmoid(c)
    w_hi, w_lo = _split_bf16(w_ref[0])
    o_ref[0] = _dot3(s, w_hi, w_lo) + b_ref[0]


def _ada_mods(c, w_ada, b_ada):
    depth, d, e = w_ada.shape
    nb = c.shape[0]
    tn = 1536
    return pl.pallas_call(
        _ada_kernel,
        grid=(depth, e // tn),
        in_specs=[
            pl.BlockSpec((nb, d), lambda l, j: (0, 0)),
            pl.BlockSpec((1, d, tn), lambda l, j: (l, 0, j)),
            pl.BlockSpec((1, 1, tn), lambda l, j: (l, 0, j)),
        ],
        out_specs=pl.BlockSpec((1, nb, tn), lambda l, j: (l, 0, j)),
        out_shape=jax.ShapeDtypeStruct((depth, nb, e), F32),
        compiler_params=_cparams("arbitrary", "arbitrary"),
        name="ada_mods",
    )(c, w_ada, b_ada.reshape(depth, 1, e))


def _hg_gate(z, lb):
    e = jnp.exp(-jnp.abs(z))
    r = 1.0 / (1.0 + e)
    er = e * r
    pos = z >= 0.0
    sig = jnp.where(pos, r, er)
    sig_neg = jnp.where(pos, er, r)
    oml = 1.0 - lb
    return oml * sig_neg, jnp.log(lb + oml * sig)


def _store_slabs(ref, slab0, val):
    for j in range(val.shape[1] // LANES):
        ref[slab0 + j] = val[:, j * LANES:(j + 1) * LANES].astype(ref.dtype)


def _store_hi_lo(pl_ref, blk, logg):
    hi, lo = _split_bf16(logg)
    pl_ref[2 * blk] = hi
    pl_ref[2 * blk + 1] = lo


def _in_proj_kernel(x_ref, mod_ref, nw_ref, w_ref, lb_ref, wup_ref, bup_ref, pa_ref, pk_ref, pl_ref):
    m = mod_ref[0]
    h = _rms(x_ref[...], nw_ref[...]) * (1.0 + m[1:2, :]) + m[0:1, :]
    hb = h.astype(BF16)
    proj = lambda lo, hi: _dot(hb, w_ref[:, lo:hi])

    def plain(c):
        _store_slabs(pa_ref, 4 * c, proj(c * 512, (c + 1) * 512))

    lb = lb_ref[...]
    for d in range(2):
        z = proj(PA_WIDTH + d * HG_WIDTH, PA_WIDTH + (d + 1) * HG_WIDTH)
        k, logf = _hg_gate(z, lb[d:d + 1, :])
        _store_slabs(pk_ref, d * HG_HEADS, k)
        for hh in range(HG_HEADS):
            _store_hi_lo(pl_ref, d * HG_HEADS + hh, logf[:, hh * HG_DK:(hh + 1) * HG_DK])
        plain(1 + d)
    lr = proj(PA_WIDTH + 2 * HG_WIDTH, PA_WIDTH + PG_WIDTH).astype(BF16)
    zg = _dot(lr, wup_ref[...]) + bup_ref[...]
    logg = (jnp.minimum(zg, 0.0) - jnp.log(1.0 + jnp.exp(-jnp.abs(zg)))) * (1.0 / GLA_GATE_NORMALIZER)
    npair = GLA_HEADS // 2
    for p in range(npair):
        for d in range(2):
            src = (p * 2 + d) * LANES
            _store_hi_lo(pl_ref, 2 * HG_HEADS + d * npair + p, logg[:, src:src + LANES])
    plain(4)
    q = proj(0, HG_WIDTH)
    _store_slabs(pa_ref, 0, q * _sigmoid(q) * (HG_DK ** -0.5))
    plain(5)
    qk = proj(3 * 512, 4 * 512)
    _store_slabs(pa_ref, 12, qk[:, :GLA_QK] * (GLA_DK ** -0.5))
    _store_slabs(pa_ref, 12 + GLA_QK // LANES, qk[:, GLA_QK:])


def _in_proj(x, mod, nw, w, lb, wup, bup, seq_len, tm):
    n, d = x.shape
    tps = seq_len // tm
    const = lambda shape: pl.BlockSpec(shape, lambda i: (0, 0))
    return pl.pallas_call(
        _in_proj_kernel,
        grid=(n // tm,),
        in_specs=[
            pl.BlockSpec((tm, d), lambda i: (i, 0)),
            pl.BlockSpec((1, 6, d), lambda i: (i // tps, 0, 0)),
            const((1, d)),
            const((d, PA_WIDTH + PG_WIDTH)),
            const((2, HG_WIDTH)),
            const((LANES, GLA_HEADS * LANES)),
            const((1, GLA_HEADS * LANES)),
        ],
        out_specs=[pl.BlockSpec((w // LANES, tm, LANES), lambda i: (0, i, 0))
                   for w in (PA_WIDTH, PK_WIDTH, PL_WIDTH)],
        out_shape=[jax.ShapeDtypeStruct((w // LANES, n, LANES), BF16) for w in (PA_WIDTH, PK_WIDTH, PL_WIDTH)],
        compiler_params=_cparams("arbitrary"),
        name="in_proj",
    )(x, mod, nw, w, lb, wup, bup)


def _block_masks(rows):
    r = jnp.arange(rows, dtype=jnp.int32)[:, None]
    c = jnp.arange(rows, dtype=jnp.int32)[None, :]
    same = (r // CHUNK) == (c // CHUNK)
    return jnp.stack([same & (c <= r), same & (c >= r)]).astype(F32)


def _chunk_slices(a, n_sub):
    return [a[c * CHUNK:(c + 1) * CHUNK] for c in range(n_sub)]


def _decay_factors(cs, fwd):
    n_sub = cs.shape[0] // CHUNK
    w = cs.shape[1] // 2
    b = cs[:, :w] + cs[:, w:]
    r = CHUNK // 2 if fwd else CHUNK // 2 - 1
    last = CHUNK - 1 if fwd else 0
    out = []
    for bc in _chunk_slices(b, n_sub):
        b_ref = bc[r:r + 1, :]
        b_last = bc[last:last + 1, :]
        out.append((jnp.exp(bc - b_ref), jnp.exp(b_ref - bc), jnp.exp(b_ref), jnp.exp(b_last - b_ref),
                    jnp.exp(b_last)))
    return out


def _stage_chains(chains):
    sums = [_dot(tri, hl) for _, _, hl, tri, _, _, _, _ in chains]
    for cs, (q_fn, k_fn, _, _, fwd, st_ref, dec_ref, d) in zip(sums, chains):
        fac = _decay_factors(cs, fwd)
        q = q_fn()
        k = k_fn()
        for c in range(len(fac)):
            sl = slice(c * CHUNK, (c + 1) * CHUNK)
            a, ainv, e_ref, e_last_ref, decay = fac[c]
            qa = q[sl] * a
            ka = k[sl] * ainv
            st_ref[0, d, sl, :] = qa.astype(BF16)
            st_ref[1, d, sl, :] = ka.astype(BF16)
            st_ref[2, d, sl, :] = (qa * e_ref).astype(BF16)
            st_ref[3, d, sl, :] = (ka * e_last_ref).astype(BF16)
            dec_ref[d, c * 8:(c + 1) * 8, :] = jnp.broadcast_to(decay, (8, decay.shape[1]))


def _mm_chains(chains):
    n_sub = chains[0][0].shape[2] // CHUNK
    heads = []
    for ci, (st_ref, dec_ref, d, v_heads, q_masks, maskf, s_ref, fwd) in enumerate(chains):
        for h, v in enumerate(v_heads):
            hm = q_masks[h]
            sel = (lambda a: a) if hm is None else (lambda a, hm=hm: jnp.where(hm, a, jnp.zeros_like(a)))
            heads.append(dict(ci=ci, st=st_ref, dec=dec_ref, d=d, h=h, v=v, vs=_chunk_slices(v, n_sub),
                              sel=sel, maskf=maskf, s_ref=s_ref, fwd=fwd))
    for e in heads:
        e["sc"] = _dot_nt(e["sel"](e["st"][0, e["d"]]), e["st"][1, e["d"]])
    for e in heads:
        e["ds"] = [_dot_tn(e["vs"][c], e["st"][3, e["d"], c * CHUNK:(c + 1) * CHUNK, :]) for c in range(n_sub)]
    for e in heads:
        p = jnp.where(e["maskf"] > 0.5, e["sc"], 0.0).astype(BF16)
        e["oi"] = _chunk_slices(_dot(p, e["v"]), n_sub)
    outs = [[] for _ in chains]
    for e in heads:
        d, st_ref, dec_ref = e["d"], e["st"], e["dec"]
        s_t = e["s_ref"][d, e["h"]]
        o = [None] * n_sub
        for c in (range(n_sub) if e["fwd"] else range(n_sub - 1, -1, -1)):
            sl = slice(c * CHUNK, (c + 1) * CHUNK)
            o[c] = e["oi"][c] + _dot_nt(e["sel"](st_ref[2, d, sl, :]), s_t.astype(BF16))
            s_t = dec_ref[d, c * 8:c * 8 + 1, :] * s_t + e["ds"][c]
        e["s_ref"][d, e["h"]] = s_t
        outs[e["ci"]].append(jnp.concatenate(o, axis=0))
    return outs


def _pipelined_steps(n_blocks, stage_fn, mm_fn, slots):
    n_pairs = n_blocks // 2

    def pair(p, fin0, fin1, last):
        stage_fn(2 * p + 1, slots[1])
        mm_fn(2 * p, slots[0], fin0)
        if not last:
            stage_fn(2 * p + 2, slots[0])
        mm_fn(2 * p + 1, slots[1], fin1)

    groups = []
    for p in range(n_pairs):
        key = (2 * p >= n_pairs, 2 * p + 1 >= n_pairs, p == n_pairs - 1)
        if groups and groups[-1][2] == key:
            groups[-1][1] = p + 1
        else:
            groups.append([p, p + 1, key])

    stage_fn(0, slots[0])
    for lo, hi, key in groups:
        if hi - lo == 1:
            pair(lo, *key)
        else:
            def body(p, carry, key=key):
                pair(p, *key)
                return carry
            lax.fori_loop(lo, hi, body, 0)


def _block_rows(step, d, n_blocks, blk_rows):
    blk = step if d == 0 else n_blocks - 1 - step
    if isinstance(blk, int):
        return pl.ds(blk * blk_rows, blk_rows)
    return pl.ds(pl.multiple_of(blk * blk_rows, blk_rows), blk_rows)


def _emit_block(outs, rows, col0, fin, acc_ref, g_ref, nw, o_ref):
    for h, o in enumerate(outs):
        if fin:
            g = g_ref[col0 + h, rows, :].astype(F32)
            o_ref[col0 + h, rows, :] = (_rms(o + acc_ref[col0 + h, rows, :], nw)
                                        * (g * _sigmoid(g))).astype(o_ref.dtype)
        else:
            acc_ref[col0 + h, rows, :] = o


def _mixer_kernel(heads_per_stream, q_ref, kf_ref, kb_ref, v_ref, g_ref, hlf_ref, hlb_ref, nw_ref, mask_ref,
                  o_ref, acc_ref, s_ref, st_a, dec_a, st_b, dec_b):
    blk_rows = mask_ref.shape[1]
    n_blocks = q_ref.shape[1] // blk_rows
    n_streams = q_ref.shape[0]
    nw = nw_ref[...]
    k_refs = (kf_ref, kb_ref)
    hl_refs = (hlf_ref, hlb_ref)
    if heads_per_stream == 1:
        q_masks = [None]
    else:
        lane = lax.broadcasted_iota(jnp.int32, (1, LANES), 1)
        q_masks = [lane < GLA_DK, lane >= GLA_DK]

    def stage_fn(step, slot):
        chains = []
        for d in range(2):
            tri = mask_ref[d].astype(BF16)
            rows = _block_rows(step, d, n_blocks, blk_rows)
            for s in range(n_streams):
                hl = jnp.concatenate([hl_refs[d][2 * s, rows, :], hl_refs[d][2 * s + 1, rows, :]], axis=1)
                chains.append((lambda rows=rows, s=s: q_ref[s, rows, :].astype(F32),
                               lambda rows=rows, s=s, d=d: k_refs[d][s, rows, :].astype(F32),
                               hl, tri, d == 0, slot[0].at[s], slot[1].at[s], d))
        _stage_chains(chains)

    def mm_fn(step, slot, fin):
        chains, where = [], []
        for d in range(2):
            rows = _block_rows(step, d, n_blocks, blk_rows)
            for s in range(n_streams):
                col0 = s * heads_per_stream
                v_heads = [v_ref[col0 + h, rows, :] for h in range(heads_per_stream)]
                chains.append((slot[0].at[s], slot[1].at[s], d, v_heads, q_masks, mask_ref[d],
                               s_ref.at[s], d == 0))
                where.append((rows, col0))
        for outs, (rows, col0) in zip(_mm_chains(chains), where):
            _emit_block(outs, rows, col0, fin, acc_ref, g_ref, nw, o_ref)

    s_ref[...] = jnp.zeros_like(s_ref)
    _pipelined_steps(n_blocks, stage_fn, mm_fn, ((st_a, dec_a), (st_b, dec_b)))


def _mixer_call(name, arrays, col_units, heads_per_stream, n_streams, n_groups, dv, out_width, nw, seq_len):
    n = arrays[0].shape[1]
    br = _mixer_block_rows(seq_len)
    vs = n_streams * heads_per_stream
    slabs = (n_streams,) * 3 + (vs, vs) + (2 * n_streams,) * 2
    in_specs = [pl.BlockSpec((k, seq_len, LANES), lambda b, j, off=off: (off + j, b, 0))
                for k, off in zip(slabs, col_units)]
    in_specs += [pl.BlockSpec((1, dv), lambda b, j: (0, 0)), pl.BlockSpec((2, br, br), lambda b, j: (0, 0, 0))]
    slot = [pltpu.VMEM((n_streams, 4, 2, br, LANES), BF16),
            pltpu.VMEM((n_streams, 2, br // CHUNK * 8, LANES), F32)]
    return pl.pallas_call(
        functools.partial(_mixer_kernel, heads_per_stream),
        grid=(n // seq_len, n_groups),
        in_specs=in_specs,
        out_specs=pl.BlockSpec((vs, seq_len, dv), lambda b, j: (j, b, 0)),
        out_shape=jax.ShapeDtypeStruct((out_width // dv, n, dv), BF16),
        scratch_shapes=[pltpu.VMEM((vs, seq_len, dv), F32),
                        pltpu.VMEM((n_streams, 2, heads_per_stream, dv, LANES), F32)] + slot + slot,
        compiler_params=pltpu.CompilerParams(dimension_semantics=("arbitrary", "arbitrary"),
                                             vmem_limit_bytes=MIXER_VMEM_LIMIT),
        name=name,
    )(*arrays, nw, _block_masks(br))


def _mixer_block_rows(seq_len):
    return min(MIXER_BLOCK_ROWS, seq_len // 2)


def _mixer_streams(seq_len, bytes_per_row_per_stream, max_streams):
    return next(ns for ns in (4, 2, 1)
                if ns <= max_streams and (ns == 1 or ns * seq_len * bytes_per_row_per_stream <= MIXER_VMEM_BUDGET))


def _hg_mixer(pa, pk, plog, nw, seq_len):
    ns = _mixer_streams(seq_len, (5 * 2 + 2 * 4 + 2) * 2 * LANES + 4 * LANES, HG_HEADS)
    ng = HG_HEADS // ns
    units = (0, 0, HG_HEADS // ns, HG_HEADS // ns, 2 * HG_HEADS // ns, 0, HG_HEADS // ns)
    return _mixer_call("hg_mixer", (pa, pk, pk, pa, pa, plog, plog), units, 1, ns, ng, HG_DK, HG_WIDTH,
                       nw, seq_len)


def _gla_mixer(pa, plog, nw, seq_len):
    npair = GLA_HEADS // 2
    ns = _mixer_streams(seq_len, (3 * 2 + 2 * 4 + 2 * 4 + 4) * 2 * LANES + 8 * LANES, npair)
    ng = npair // ns
    units = (12 // ns, 14 // ns, 14 // ns, 8 // ns, 10 // ns, 2 * HG_HEADS // ns, (2 * HG_HEADS + npair) // ns)
    return _mixer_call("gla_mixer", (pa, pa, pa, pa, pa, plog, plog), units, 2, ns, ng, GLA_DV, GLA_WIDTH,
                       nw, seq_len)


def _out_proj_body(x_ref, mh_ref, mg_ref, mod_ref, w_ref, nw_ref):
    m = mod_ref[0]
    mixed = jnp.concatenate([mh_ref[j] for j in range(mh_ref.shape[0])]
                            + [mg_ref[j] for j in range(mg_ref.shape[0])], axis=1)
    y = _dot(mixed, w_ref[...])
    x = x_ref[...] + m[2:3, :] * y
    h = _rms(x, nw_ref[...]) * (1.0 + m[4:5, :]) + m[3:4, :]
    return x, h


def _out_proj_kernel(x_ref, mh_ref, mg_ref, mod_ref, w_ref, nw_ref, xo_ref, h_ref):
    x, h = _out_proj_body(x_ref, mh_ref, mg_ref, mod_ref, w_ref, nw_ref)
    xo_ref[...] = x
    h_ref[...] = h.astype(BF16)


def _out_proj_route_kernel(x_ref, mh_ref, mg_ref, mod_ref, w_ref, nw_ref, wr_ref, br_ref,
                           xo_ref, h_ref, route_ref):
    x, h = _out_proj_body(x_ref, mh_ref, mg_ref, mod_ref, w_ref, nw_ref)
    xo_ref[...] = x
    h_ref[...] = _pack_bf16_pair(h)
    wr_hi, wr_lo = _split_bf16(wr_ref[...])
    logits = _dot3(h, wr_hi, wr_lo) + br_ref[...]
    lane = lax.broadcasted_iota(jnp.int32, logits.shape, 1).astype(F32)
    neg = jnp.float32(-jnp.inf)
    logits = jnp.where(lane < N_EXPERTS, logits, neg)
    m1 = jnp.max(logits, axis=-1, keepdims=True)
    i1 = jnp.min(jnp.where(logits == m1, lane, float(LANES)), axis=-1, keepdims=True)
    rest = jnp.where(lane == i1, neg, logits)
    m2 = jnp.max(rest, axis=-1, keepdims=True)
    i2 = jnp.min(jnp.where(rest == m2, lane, float(LANES)), axis=-1, keepdims=True)
    e2 = jnp.exp(m2 - m1)
    den = 1.0 / (1.0 + e2)
    route = jnp.where(lane == 0.0, i1,
                      jnp.where(lane == 1.0, i2,
                                jnp.where(lane == 2.0, den, jnp.where(lane == 3.0, e2 * den, 0.0))))
    route_ref[...] = route


def _out_proj(x, mh, mg, mod, w, nw, seq_len, tm, router=None):
    n, d = x.shape
    tps = seq_len // tm
    in_specs = [
        pl.BlockSpec((tm, d), lambda i: (i, 0)),
        pl.BlockSpec((HG_WIDTH // LANES, tm, LANES), lambda i: (0, i, 0)),
        pl.BlockSpec((GLA_WIDTH // LANES, tm, LANES), lambda i: (0, i, 0)),
        pl.BlockSpec((1, 6, d), lambda i: (i // tps, 0, 0)),
        pl.BlockSpec((d, d), lambda i: (0, 0)),
        pl.BlockSpec((1, d), lambda i: (0, 0)),
    ]
    out_specs = [pl.BlockSpec((tm, d), lambda i: (i, 0)), pl.BlockSpec((tm, d), lambda i: (i, 0))]
    out_shape = [jax.ShapeDtypeStruct((n, d), F32), jax.ShapeDtypeStruct((n, d), BF16)]
    args = [x, mh, mg, mod, w, nw]
    body = _out_proj_kernel
    if router is not None:
        in_specs += [pl.BlockSpec((d, LANES), lambda i: (0, 0)), pl.BlockSpec((1, LANES), lambda i: (0, 0))]
        out_specs.append(pl.BlockSpec((tm, LANES), lambda i: (i, 0)))
        out_specs[1] = pl.BlockSpec((tm, d // 2), lambda i: (i, 0))
        out_shape[1] = jax.ShapeDtypeStruct((n, d // 2), jnp.uint32)
        out_shape.append(jax.ShapeDtypeStruct((n, LANES), F32))
        args += list(router)
        body = _out_proj_route_kernel
    return pl.pallas_call(
        body,
        grid=(n // tm,),
        in_specs=in_specs,
        out_specs=out_specs,
        out_shape=out_shape,
        compiler_params=_cparams("arbitrary"),
        name="out_proj",
    )(*args)


def _swiglu_step(xb, w1_ref, w3_ref, w2_ref, acc_ref):
    f = pl.program_id(1)
    a = _dot(xb, w1_ref[0])
    b = _dot(xb, w3_ref[0])
    part = _dot((a * _sigmoid(a) * b).astype(BF16), w2_ref[0])

    @pl.when(f == 0)
    def _():
        acc_ref[...] = part

    @pl.when(f > 0)
    def _():
        acc_ref[...] += part


def _dense_ffn_kernel(h_ref, x_ref, mod_ref, w1_ref, w3_ref, w2_ref, o_ref, acc_ref):
    _swiglu_step(h_ref[...], w1_ref, w3_ref, w2_ref, acc_ref)

    @pl.when(pl.program_id(1) == pl.num_programs(1) - 1)
    def _():
        o_ref[...] = x_ref[...] + mod_ref[0][5:6, :] * acc_ref[...]


def _dense_ffn(h, x, mod, w1, w3, w2, seq_len, tm, tf):
    n, d = x.shape
    ff = w1.shape[-1]
    tps = seq_len // tm
    return pl.pallas_call(
        _dense_ffn_kernel,
        grid=(n // tm, ff // tf),
        in_specs=[
            pl.BlockSpec((tm, d), lambda i, f: (i, 0)),
            pl.BlockSpec((tm, d), lambda i, f: (i, 0)),
            pl.BlockSpec((1, 6, d), lambda i, f: (i // tps, 0, 0)),
            pl.BlockSpec((1, d, tf), lambda i, f: (0, 0, f)),
            pl.BlockSpec((1, d, tf), lambda i, f: (0, 0, f)),
            pl.BlockSpec((1, tf, d), lambda i, f: (0, f, 0)),
        ],
        out_specs=pl.BlockSpec((tm, d), lambda i, f: (i, 0)),
        out_shape=jax.ShapeDtypeStruct((n, d), F32),
        scratch_shapes=[pltpu.VMEM((tm, d), F32)],
        compiler_params=_cparams("arbitrary", "arbitrary"),
        name="dense_ffn",
    )(h, x, mod, w1, w3, w2)


def _expert_ffn_kernel(be_ref, nu_ref, nv_ref, xs_ref, w1_ref, w3_ref, w2_ref, o_ref, acc_ref):
    i = pl.program_id(0)
    last = pl.program_id(1) == pl.num_programs(1) - 1
    used = i < nu_ref[0]

    @pl.when(used)
    def _():
        row = lax.broadcasted_iota(jnp.int32, (xs_ref.shape[0], 1), 0)
        lo, hi = _unpack_bf16_pair(jnp.where(row < nv_ref[i], xs_ref[...], jnp.uint32(0)))
        xb = jnp.concatenate([lo.astype(BF16), hi.astype(BF16)], axis=1)
        _swiglu_step(xb, w1_ref, w3_ref, w2_ref, acc_ref)

    @pl.when(jnp.logical_and(used, last))
    def _():
        o_ref[...] = _pack_bf16_pair(acc_ref[...])

    @pl.when(jnp.logical_and(jnp.logical_not(used), last))
    def _():
        o_ref[...] = jnp.zeros_like(o_ref)


def _expert_ffn(xs, block_e, n_used, n_valid, w1, w3, w2, tm, tf):
    rows = xs.shape[0]
    d, ff = w1.shape[1:]
    nf = ff // tf

    def row_idx(i, f, be, nu, nv):
        return (jnp.minimum(i, nu[0] - 1), 0)

    def fsel(i, f, nu):
        return jnp.where(i < nu[0], f, nf - 1)

    wmode = dict(pipeline_mode=pl.Buffered(1)) if nf == 1 else {}
    grid_spec = pltpu.PrefetchScalarGridSpec(
        num_scalar_prefetch=3,
        grid=(rows // tm, nf),
        in_specs=[
            pl.BlockSpec((tm, d // 2), row_idx),
            pl.BlockSpec((1, d, tf), lambda i, f, be, nu, nv: (be[i], 0, fsel(i, f, nu)), **wmode),
            pl.BlockSpec((1, d, tf), lambda i, f, be, nu, nv: (be[i], 0, fsel(i, f, nu)), **wmode),
            pl.BlockSpec((1, tf, d), lambda i, f, be, nu, nv: (be[i], fsel(i, f, nu), 0), **wmode),
        ],
        out_specs=pl.BlockSpec((tm, d // 2), lambda i, f, be, nu, nv: (i, 0)),
        scratch_shapes=[pltpu.VMEM((tm, d), F32)],
    )
    return pl.pallas_call(
        _expert_ffn_kernel,
        grid_spec=grid_spec,
        out_shape=jax.ShapeDtypeStruct((rows, d // 2), jnp.uint32),
        compiler_params=_cparams("arbitrary", "arbitrary"),
        name="expert_ffn",
    )(block_e, n_used, n_valid, xs, w1, w3, w2)


def _scatter_rows(src, dest_a, dest_b, n_out):
    n, d = src.shape
    n_workers = SC_CORES * SC_SUBCORES
    per_worker = n // n_workers
    chunk = next(c for c in (32, 16, 8) if per_worker % c == 0)
    n_chunks = per_worker // chunk
    mesh = plsc.VectorSubcoreMesh(core_axis_name="c", subcore_axis_name="s")

    @functools.partial(
        pl.kernel, mesh=mesh, out_type=jax.ShapeDtypeStruct((n_out, d), src.dtype),
        scratch_types=[pltpu.VMEM((chunk,), jnp.int32), pltpu.VMEM((chunk,), jnp.int32),
                       pltpu.VMEM((chunk, d), src.dtype)])
    def scatter_kernel(src_hbm, da_hbm, db_hbm, out_hbm, ia_v, ib_v, rows_v):
        worker = lax.axis_index("s") * SC_CORES + lax.axis_index("c")
        base = worker * per_worker

        @pl.loop(0, n_chunks)
        def _(j):
            off = pl.multiple_of(base + j * chunk, chunk)
            pltpu.sync_copy(da_hbm.at[pl.ds(off, chunk)], ia_v)
            pltpu.sync_copy(db_hbm.at[pl.ds(off, chunk)], ib_v)
            pltpu.sync_copy(src_hbm.at[pl.ds(off, chunk)], rows_v)
            pltpu.sync_copy(rows_v, out_hbm.at[ia_v])
            pltpu.sync_copy(rows_v, out_hbm.at[ib_v])

    return scatter_kernel(src, dest_a, dest_b)


def _gather_rows(table, idx):
    n_rows = idx.shape[0]
    d = table.shape[1]
    n_workers = SC_CORES * SC_SUBCORES
    per_worker = n_rows // n_workers
    chunk = next(c for c in (64, 32, 16, 8) if per_worker % c == 0)
    n_chunks = per_worker // chunk
    mesh = plsc.VectorSubcoreMesh(core_axis_name="c", subcore_axis_name="s")

    @functools.partial(
        pl.kernel, mesh=mesh, out_type=jax.ShapeDtypeStruct((n_rows, d), table.dtype),
        scratch_types=[pltpu.VMEM((chunk,), jnp.int32), pltpu.VMEM((chunk, d), table.dtype),
                       pltpu.SemaphoreType.DMA])
    def gather_kernel(table_hbm, idx_hbm, out_hbm, idx_v, rows_v, sem):
        worker = lax.axis_index("s") * SC_CORES + lax.axis_index("c")
        base = worker * per_worker

        @pl.loop(0, n_chunks)
        def _(j):
            off = pl.multiple_of(base + j * chunk, chunk)
            pltpu.sync_copy(idx_hbm.at[pl.ds(off, chunk)], idx_v)
            pltpu.async_copy(table_hbm.at[idx_v], rows_v, sem).wait()
            pltpu.sync_copy(rows_v, out_hbm.at[pl.ds(off, chunk)])

    return gather_kernel(table, idx)


def _combine_body(x_ref, ya_ref, yb_ref, route_ref, mod_ref):
    r = route_ref[...]
    ga, gb = r[:, TOP_K:TOP_K + 1], r[:, TOP_K + 1:TOP_K + 2]
    ya = _unpack_bf16_pair(ya_ref[...])
    yb = _unpack_bf16_pair(yb_ref[...])
    f = jnp.concatenate([ga * ya[0] + gb * yb[0], ga * ya[1] + gb * yb[1]], axis=1)
    return x_ref[...] + mod_ref[0][5:6, :] * f


def _combine_kernel(x_ref, ya_ref, yb_ref, route_ref, mod_ref, o_ref):
    o_ref[...] = _combine_body(x_ref, ya_ref, yb_ref, route_ref, mod_ref)


def _combine_norm_kernel(x_ref, ya_ref, yb_ref, route_ref, mod_ref, nw_ref, o_ref):
    o_ref[...] = _rms(_combine_body(x_ref, ya_ref, yb_ref, route_ref, mod_ref), nw_ref[...])


def _combine(x, y2, route, mod, seq_len, tm, final_nw=None):
    n, d = x.shape
    tps = seq_len // tm
    nt = n // tm
    in_specs = [
        pl.BlockSpec((tm, d), lambda i: (i, 0)),
        pl.BlockSpec((tm, d // 2), lambda i: (i, 0)),
        pl.BlockSpec((tm, d // 2), lambda i: (i + nt, 0)),
        pl.BlockSpec((tm, LANES), lambda i: (i, 0)),
        pl.BlockSpec((1, 6, d), lambda i: (i // tps, 0, 0)),
    ]
    args = [x, y2, y2, route, mod]
    body = _combine_kernel
    if final_nw is not None:
        in_specs.append(pl.BlockSpec((1, d), lambda i: (0, 0)))
        args.append(final_nw)
        body = _combine_norm_kernel
    return pl.pallas_call(
        body,
        grid=(n // tm,),
        in_specs=in_specs,
        out_specs=pl.BlockSpec((tm, d), lambda i: (i, 0)),
        out_shape=jax.ShapeDtypeStruct((n, d), F32),
        compiler_params=_cparams("arbitrary"),
        name="moe_combine",
    )(*args)


def _final_norm_kernel(x_ref, w_ref, o_ref):
    o_ref[...] = _rms(x_ref[...], w_ref[...])


def _final_norm(x, w, tm):
    n, d = x.shape
    return pl.pallas_call(
        _final_norm_kernel,
        grid=(n // tm,),
        in_specs=[pl.BlockSpec((tm, d), lambda i: (i, 0)), pl.BlockSpec((1, d), lambda i: (0, 0))],
        out_specs=pl.BlockSpec((tm, d), lambda i: (i, 0)),
        out_shape=jax.ShapeDtypeStruct((n, d), F32),
        compiler_params=_cparams("arbitrary"),
        name="final_norm",
    )(x, w)


def _moe(h, x, mod, route, w1, w3, w2, seq_len, tm, tf, tm_c, final_nw):
    n, d = x.shape
    n_asg = n * TOP_K
    flat_e = route[:, :TOP_K].astype(jnp.int32).reshape(-1)
    onehot = (flat_e[:, None] == jnp.arange(N_EXPERTS, dtype=jnp.int32)[None, :]).astype(jnp.int32)
    csum = jnp.cumsum(onehot, axis=0)
    counts = csum[-1]
    rank = jnp.take_along_axis(csum, flat_e[:, None], axis=1)[:, 0] - 1
    pcounts = (counts + tm - 1) // tm * tm
    pend = jnp.cumsum(pcounts)
    pstart = pend - pcounts
    dest = pstart[flat_e] + rank
    n_blocks = -(-n_asg // tm) + N_EXPERTS
    block_start = jnp.arange(n_blocks, dtype=pend.dtype) * tm
    block_e = jnp.minimum(jnp.sum((pend[None, :] <= block_start[:, None]).astype(jnp.int32), axis=1),
                          N_EXPERTS - 1)
    n_used = (pend[-1:] // tm).astype(jnp.int32)
    n_valid = jnp.clip((pstart + counts)[block_e] - block_start, 0, tm).astype(jnp.int32)
    block_e = jnp.where(jnp.arange(n_blocks) < n_used[0], block_e, block_e[jnp.maximum(n_used[0] - 1, 0)])
    dest_ab = dest.reshape(n, TOP_K).T
    xs = _scatter_rows(h, dest_ab[0], dest_ab[1], n_blocks * tm)
    ys = _expert_ffn(xs, block_e, n_used, n_valid, w1, w3, w2, tm, tf)
    y2 = _gather_rows(ys, dest_ab.reshape(-1))
    return _combine(x, y2, route, mod, seq_len, tm_c, final_nw)


def _prep_weights(w_in, gla_w_gk_up, gla_b_gk, hg_lb_logits):
    depth = w_in.shape[0]
    sizes = (HG_WIDTH,) * 5 + (GLA_QK, GLA_QK, GLA_WIDTH, GLA_WIDTH, GLA_GATE_RANK, GLA_GATE_RANK)
    offs = [0]
    for s in sizes:
        offs.append(offs[-1] + s)
    seg = lambda k: w_in[:, :, offs[k]:offs[k + 1]]
    hq, hf_f, hf_b, hi, hg, gq, gk, gv, gg, lr_f, lr_b = [seg(k) for k in range(11)]
    pad = jnp.zeros(w_in.shape[:2] + (LANES - 2 * GLA_GATE_RANK,), w_in.dtype)
    w_perm = jnp.concatenate([hq, hi, hg, gq, gk, gv, gg, hf_f, hf_b, lr_f, lr_b, pad], axis=-1).astype(BF16)

    r = GLA_GATE_RANK
    wup = jnp.zeros((depth, LANES, GLA_HEADS // 2, 2, LANES), F32)
    bup = jnp.zeros((depth, GLA_HEADS // 2, 2, LANES), F32)
    for dd in range(2):
        wup = wup.at[:, dd * r:(dd + 1) * r, :, dd, :].set(
            gla_w_gk_up[:, dd].reshape(depth, r, GLA_HEADS // 2, LANES))
        bup = bup.at[:, :, dd, :].set(gla_b_gk[:, dd].reshape(depth, GLA_HEADS // 2, LANES))
    wup = wup.reshape(depth, LANES, GLA_HEADS * LANES).astype(BF16)
    bup = bup.reshape(depth, 1, GLA_HEADS * LANES)

    lb = jnp.cumsum(jax.nn.softmax(hg_lb_logits.astype(F32), axis=0), axis=0)
    lb = lb - lb[0]
    return w_perm, wup, bup, lb


def _pick(pref, total):
    t = min(pref, total)
    while total % t:
        t //= 2
    return t


def _trunk(x3, mods, weights):
    (norm1_w, w_perm, lb, wup, bup, hg_norm_w, gla_norm_w, w_out, norm2_w, w_ff1, w_ff3, w_ff2,
     w_router, b_router, w_e1, w_e3, w_e2, final_norm_w) = weights
    nb, seq_len, d = x3.shape
    depth = norm1_w.shape[0]
    n = nb * seq_len
    x = x3.reshape(n, d)
    tm = _pick(512, seq_len)
    tm_ffn = _pick(512, seq_len)
    tm_moe = _pick(512, n * TOP_K)
    for l in range(depth):
        mod = mods[l]
        pa, pk, plog = _in_proj(x, mod, norm1_w[l][None, :], w_perm[l], lb[l], wup[l], bup[l], seq_len,
                                _pick(1024, seq_len))
        mh = _hg_mixer(pa, pk, plog, hg_norm_w[l][None, :], seq_len)
        mg = _gla_mixer(pa, plog, gla_norm_w[l][None, :], seq_len)
        m = l // 2
        if l % 2 == 0:
            x, h = _out_proj(x, mh, mg, mod, w_out[l], norm2_w[l][None, :], seq_len, tm)
            x = _dense_ffn(h, x, mod, w_ff1[m:m + 1], w_ff3[m:m + 1], w_ff2[m:m + 1], seq_len, tm_ffn,
                           w_ff1.shape[-1])
        else:
            wr = jnp.pad(w_router[m], ((0, 0), (0, LANES - N_EXPERTS)))
            br = jnp.pad(b_router[m], (0, LANES - N_EXPERTS))[None, :]
            x, h, route = _out_proj(x, mh, mg, mod, w_out[l], norm2_w[l][None, :], seq_len, tm,
                                    router=(wr, br))
            final_nw = final_norm_w[None, :] if l == depth - 1 else None
            x = _moe(h, x, mod, route, w_e1[m], w_e3[m], w_e2[m], seq_len, tm_moe,
                     w_e1.shape[-1], tm, final_nw)
    if depth % 2 == 1:
        x = _final_norm(x, final_norm_w[None, :], tm)
    return x.reshape(nb, seq_len, d)


def kernel(x_prompt, x_sample, c_prompt, c_sample, w_ada, b_ada, norm1_w, w_in, hg_lb_logits, gla_w_gk_up, gla_b_gk, hg_norm_w, gla_norm_w, w_out, norm2_w, w_ff1, w_ff3, w_ff2, w_router, b_router, w_e1, w_e3, w_e2, final_norm_w):
    depth, d = norm1_w.shape
    w_perm, wup, bup, lb = _prep_weights(w_in, gla_w_gk_up, gla_b_gk, hg_lb_logits)
    nbp = c_prompt.shape[0]
    mods = _ada_mods(jnp.concatenate([c_prompt, c_sample], axis=0), w_ada, b_ada)
    mods = mods.reshape(depth, mods.shape[1], 6, d)
    weights = (norm1_w, w_perm, lb, wup, bup, hg_norm_w, gla_norm_w, w_out.astype(BF16), norm2_w,
               w_ff1.astype(BF16), w_ff3.astype(BF16), w_ff2.astype(BF16), w_router, b_router,
               w_e1.astype(BF16), w_e3.astype(BF16), w_e2.astype(BF16), final_norm_w)
    y_prompt = _trunk(x_prompt, mods[:, :nbp], weights)
    y_sample = _trunk(x_sample, mods[:, nbp:], weights)
    return (y_prompt, y_sample)
```

```python
import functools

import jax
import jax.numpy as jnp
from jax import lax
from jax.experimental import pallas as pl
from jax.experimental.pallas import tpu as pltpu
from jax.experimental.pallas import tpu_sc as plsc

F32 = jnp.float32
BF16 = jnp.bfloat16

D_MODEL = 1024
HG_HEADS = 4
HG_DK = 128
HG_WIDTH = 512
GLA_HEADS = 4
GLA_DK = 64
GLA_DV = 128
GLA_QK = 256
GLA_WIDTH = 512
GLA_GATE_RANK = 16
GLA_GATE_NORMALIZER = 16.0
CHUNK = 64
N_EXPERTS = 8
TOP_K = 2
EPS = 1e-6
LANES = 128

PA_WIDTH = 3 * HG_WIDTH + 2 * GLA_QK + 2 * GLA_WIDTH
PG_WIDTH = 2 * HG_WIDTH + LANES
PK_WIDTH = 2 * HG_WIDTH
PL_WIDTH = (2 * HG_HEADS + 2 * (GLA_HEADS // 2)) * 2 * LANES

VMEM_LIMIT = 48 * 1024 * 1024
SC_CORES = 2
SC_SUBCORES = 16
MIXER_BLOCK_ROWS = 128
MIXER_VMEM_LIMIT = 56 * 1024 * 1024
MIXER_VMEM_BUDGET = 46 * 1024 * 1024


def _cparams(*sem):
    return pltpu.CompilerParams(dimension_semantics=sem, vmem_limit_bytes=VMEM_LIMIT)


def _sigmoid(x):
    return 1.0 / (1.0 + jnp.exp(-x))


def _split_bf16(a):
    hi = a.astype(BF16)
    lo = (a - hi.astype(F32)).astype(BF16)
    return hi, lo


def _dot(a, b):
    return jnp.dot(a, b, preferred_element_type=F32)


def _dot_nt(a, b):
    return lax.dot_general(a, b, (((1,), (1,)), ((), ())), preferred_element_type=F32)


def _dot_tn(a, b):
    return lax.dot_general(a, b, (((0,), (0,)), ((), ())), preferred_element_type=F32)


def _dot3(a, b_hi, b_lo):
    a_hi, a_lo = _split_bf16(a)
    return _dot(a_hi, b_hi) + (_dot(a_hi, b_lo) + _dot(a_lo, b_hi))


def _pack_bf16_pair(v):
    w = v.shape[1] // 2
    lo = lax.bitcast_convert_type(v[:, :w].astype(BF16).astype(F32), jnp.uint32)
    hi = lax.bitcast_convert_type(v[:, w:].astype(BF16).astype(F32), jnp.uint32)
    return (lo >> 16) | hi


def _unpack_bf16_pair(u):
    lo = lax.bitcast_convert_type(u << 16, F32)
    hi = lax.bitcast_convert_type(u & jnp.uint32(0xFFFF0000), F32)
    return lo, hi


def _rms(x, w):
    ms = jnp.mean(x * x, axis=-1, keepdims=True)
    return x * lax.rsqrt(ms + EPS) * w


def _ada_kernel(c_ref, w_ref, b_ref, o_ref):
    c = c_ref[...]
    s = c * _sigmoid(c)
    w_hi, w_lo = _split_bf16(w_ref[0])
    o_ref[0] = _dot3(s, w_hi, w_lo) + b_ref[0]


def _ada_mods(c, w_ada, b_ada):
    depth, d, e = w_ada.shape
    nb = c.shape[0]
    tn = 1536
    return pl.pallas_call(
        _ada_kernel,
        grid=(depth, e // tn),
        in_specs=[
            pl.BlockSpec((nb, d), lambda l, j: (0, 0)),
            pl.BlockSpec((1, d, tn), lambda l, j: (l, 0, j)),
            pl.BlockSpec((1, 1, tn), lambda l, j: (l, 0, j)),
        ],
        out_specs=pl.BlockSpec((1, nb, tn), lambda l, j: (l, 0, j)),
        out_shape=jax.ShapeDtypeStruct((depth, nb, e), F32),
        compiler_params=_cparams("arbitrary", "arbitrary"),
        name="ada_mods",
    )(c, w_ada, b_ada.reshape(depth, 1, e))


def _hg_gate(z, lb):
    e = jnp.exp(-jnp.abs(z))
    r = 1.0 / (1.0 + e)
    er = e * r
    pos = z >= 0.0
    sig = jnp.where(pos, r, er)
    sig_neg = jnp.where(pos, er, r)
    oml = 1.0 - lb
    return oml * sig_neg, jnp.log(lb + oml * sig)


def _store_slabs(ref, slab0, val):
    for j in range(val.shape[1] // LANES):
        ref[slab0 + j] = val[:, j * LANES:(j + 1) * LANES].astype(ref.dtype)


def _store_hi_lo(pl_ref, blk, logg):
    hi, lo = _split_bf16(logg)
    pl_ref[2 * blk] = hi
    pl_ref[2 * blk + 1] = lo


def _in_proj_kernel(x_ref, mod_ref, nw_ref, w_ref, lb_ref, wup_ref, bup_ref, pa_ref, pk_ref, pl_ref):
    m = mod_ref[0]
    h = _rms(x_ref[...], nw_ref[...]) * (1.0 + m[1:2, :]) + m[0:1, :]
    hb = h.astype(BF16)
    proj = lambda lo, hi: _dot(hb, w_ref[:, lo:hi])

    def plain(c):
        _store_slabs(pa_ref, 4 * c, proj(c * 512, (c + 1) * 512))

    lb = lb_ref[...]
    for d in range(2):
        z = proj(PA_WIDTH + d * HG_WIDTH, PA_WIDTH + (d + 1) * HG_WIDTH)
        k, logf = _hg_gate(z, lb[d:d + 1, :])
        _store_slabs(pk_ref, d * HG_HEADS, k)
        for hh in range(HG_HEADS):
            _store_hi_lo(pl_ref, d * HG_HEADS + hh, logf[:, hh * HG_DK:(hh + 1) * HG_DK])
        plain(1 + d)
    lr = proj(PA_WIDTH + 2 * HG_WIDTH, PA_WIDTH + PG_WIDTH).astype(BF16)
    zg = _dot(lr, wup_ref[...]) + bup_ref[...]
    logg = (jnp.minimum(zg, 0.0) - jnp.log(1.0 + jnp.exp(-jnp.abs(zg)))) * (1.0 / GLA_GATE_NORMALIZER)
    npair = GLA_HEADS // 2
    for p in range(npair):
        for d in range(2):
            src = (p * 2 + d) * LANES
            _store_hi_lo(pl_ref, 2 * HG_HEADS + d * npair + p, logg[:, src:src + LANES])
    plain(4)
    q = proj(0, HG_WIDTH)
    _store_slabs(pa_ref, 0, q * _sigmoid(q) * (HG_DK ** -0.5))
    plain(5)
    qk = proj(3 * 512, 4 * 512)
    _store_slabs(pa_ref, 12, qk[:, :GLA_QK] * (GLA_DK ** -0.5))
    _store_slabs(pa_ref, 12 + GLA_QK // LANES, qk[:, GLA_QK:])


def _in_proj(x, mod, nw, w, lb, wup, bup, seq_len, tm):
    n, d = x.shape
    tps = seq_len // tm
    const = lambda shape: pl.BlockSpec(shape, lambda i: (0, 0))
    return pl.pallas_call(
        _in_proj_kernel,
        grid=(n // tm,),
        in_specs=[
            pl.BlockSpec((tm, d), lambda i: (i, 0)),
            pl.BlockSpec((1, 6, d), lambda i: (i // tps, 0, 0)),
            const((1, d)),
            const((d, PA_WIDTH + PG_WIDTH)),
            const((2, HG_WIDTH)),
            const((LANES, GLA_HEADS * LANES)),
            const((1, GLA_HEADS * LANES)),
        ],
        out_specs=[pl.BlockSpec((w // LANES, tm, LANES), lambda i: (0, i, 0))
                   for w in (PA_WIDTH, PK_WIDTH, PL_WIDTH)],
        out_shape=[jax.ShapeDtypeStruct((w // LANES, n, LANES), BF16) for w in (PA_WIDTH, PK_WIDTH, PL_WIDTH)],
        compiler_params=_cparams("arbitrary"),
        name="in_proj",
    )(x, mod, nw, w, lb, wup, bup)


def _block_masks(rows):
    r = jnp.arange(rows, dtype=jnp.int32)[:, None]
    c = jnp.arange(rows, dtype=jnp.int32)[None, :]
    same = (r // CHUNK) == (c // CHUNK)
    return jnp.stack([same & (c <= r), same & (c >= r)]).astype(F32)


def _chunk_slices(a, n_sub):
    return [a[c * CHUNK:(c + 1) * CHUNK] for c in range(n_sub)]


def _decay_factors(cs, fwd):
    n_sub = cs.shape[0] // CHUNK
    w = cs.shape[1] // 2
    b = cs[:, :w] + cs[:, w:]
    r = CHUNK // 2 if fwd else CHUNK // 2 - 1
    last = CHUNK - 1 if fwd else 0
    out = []
    for bc in _chunk_slices(b, n_sub):
        b_ref = bc[r:r + 1, :]
        b_last = bc[last:last + 1, :]
        out.append((jnp.exp(bc - b_ref), jnp.exp(b_ref - bc), jnp.exp(b_ref), jnp.exp(b_last - b_ref),
                    jnp.exp(b_last)))
    return out


def _stage_chains(chains):
    sums = [_dot(tri, hl) for _, _, hl, tri, _, _, _, _ in chains]
    for cs, (q_fn, k_fn, _, _, fwd, st_ref, dec_ref, d) in zip(sums, chains):
        fac = _decay_factors(cs, fwd)
        q = q_fn()
        k = k_fn()
        for c in range(len(fac)):
            sl = slice(c * CHUNK, (c + 1) * CHUNK)
            a, ainv, e_ref, e_last_ref, decay = fac[c]
            qa = q[sl] * a
            ka = k[sl] * ainv
            st_ref[0, d, sl, :] = qa.astype(BF16)
            st_ref[1, d, sl, :] = ka.astype(BF16)
            st_ref[2, d, sl, :] = (qa * e_ref).astype(BF16)
            st_ref[3, d, sl, :] = (ka * e_last_ref).astype(BF16)
            dec_ref[d, c * 8:(c + 1) * 8, :] = jnp.broadcast_to(decay, (8, decay.shape[1]))


def _mm_chains(chains):
    n_sub = chains[0][0].shape[2] // CHUNK
    heads = []
    for ci, (st_ref, dec_ref, d, v_heads, q_masks, maskf, s_ref, fwd) in enumerate(chains):
        for h, v in enumerate(v_heads):
            hm = q_masks[h]
            sel = (lambda a: a) if hm is None else (lambda a, hm=hm: jnp.where(hm, a, jnp.zeros_like(a)))
            heads.append(dict(ci=ci, st=st_ref, dec=dec_ref, d=d, h=h, v=v, vs=_chunk_slices(v, n_sub),
                              sel=sel, maskf=maskf, s_ref=s_ref, fwd=fwd))
    for e in heads:
        e["sc"] = _dot_nt(e["sel"](e["st"][0, e["d"]]), e["st"][1, e["d"]])
    for e in heads:
        e["ds"] = [_dot_tn(e["vs"][c], e["st"][3, e["d"], c * CHUNK:(c + 1) * CHUNK, :]) for c in range(n_sub)]
    for e in heads:
        p = jnp.where(e["maskf"] > 0.5, e["sc"], 0.0).astype(BF16)
        e["oi"] = _chunk_slices(_dot(p, e["v"]), n_sub)
    outs = [[] for _ in chains]
    for e in heads:
        d, st_ref, dec_ref = e["d"], e["st"], e["dec"]
        s_t = e["s_ref"][d, e["h"]]
        o = [None] * n_sub
        for c in (range(n_sub) if e["fwd"] else range(n_sub - 1, -1, -1)):
            sl = slice(c * CHUNK, (c + 1) * CHUNK)
            o[c] = e["oi"][c] + _dot_nt(e["sel"](st_ref[2, d, sl, :]), s_t.astype(BF16))
            s_t = dec_ref[d, c * 8:c * 8 + 1, :] * s_t + e["ds"][c]
        e["s_ref"][d, e["h"]] = s_t
        outs[e["ci"]].append(jnp.concatenate(o, axis=0))
    return outs


def _pipelined_steps(n_blocks, stage_fn, mm_fn, slots):
    n_pairs = n_blocks // 2

    def pair(p, fin0, fin1, last):
        stage_fn(2 * p + 1, slots[1])
        mm_fn(2 * p, slots[0], fin0)
        if not last:
            stage_fn(2 * p + 2, slots[0])
        mm_fn(2 * p + 1, slots[1], fin1)

    groups = []
    for p in range(n_pairs):
        key = (2 * p >= n_pairs, 2 * p + 1 >= n_pairs, p == n_pairs - 1)
        if groups and groups[-1][2] == key:
            groups[-1][1] = p + 1
        else:
            groups.append([p, p + 1, key])

    stage_fn(0, slots[0])
    for lo, hi, key in groups:
        if hi - lo == 1:
            pair(lo, *key)
        else:
            def body(p, carry, key=key):
                pair(p, *key)
                return carry
            lax.fori_loop(lo, hi, body, 0)


def _block_rows(step, d, n_blocks, blk_rows):
    blk = step if d == 0 else n_blocks - 1 - step
    if isinstance(blk, int):
        return pl.ds(blk * blk_rows, blk_rows)
    return pl.ds(pl.multiple_of(blk * blk_rows, blk_rows), blk_rows)


def _emit_block(outs, rows, col0, fin, acc_ref, g_ref, nw, o_ref):
    for h, o in enumerate(outs):
        if fin:
            g = g_ref[col0 + h, rows, :].astype(F32)
            o_ref[col0 + h, rows, :] = (_rms(o + acc_ref[col0 + h, rows, :], nw)
                                        * (g * _sigmoid(g))).astype(o_ref.dtype)
        else:
            acc_ref[col0 + h, rows, :] = o


def _mixer_kernel(heads_per_stream, q_ref, kf_ref, kb_ref, v_ref, g_ref, hlf_ref, hlb_ref, nw_ref, mask_ref,
                  o_ref, acc_ref, s_ref, st_a, dec_a, st_b, dec_b):
    blk_rows = mask_ref.shape[1]
    n_blocks = q_ref.shape[1] // blk_rows
    n_streams = q_ref.shape[0]
    nw = nw_ref[...]
    k_refs = (kf_ref, kb_ref)
    hl_refs = (hlf_ref, hlb_ref)
    if heads_per_stream == 1:
        q_masks = [None]
    else:
        lane = lax.broadcasted_iota(jnp.int32, (1, LANES), 1)
        q_masks = [lane < GLA_DK, lane >= GLA_DK]

    def stage_fn(step, slot):
        chains = []
        for d in range(2):
            tri = mask_ref[d].astype(BF16)
            rows = _block_rows(step, d, n_blocks, blk_rows)
            for s in range(n_streams):
                hl = jnp.concatenate([hl_refs[d][2 * s, rows, :], hl_refs[d][2 * s + 1, rows, :]], axis=1)
                chains.append((lambda rows=rows, s=s: q_ref[s, rows, :].astype(F32),
                               lambda rows=rows, s=s, d=d: k_refs[d][s, rows, :].astype(F32),
                               hl, tri, d == 0, slot[0].at[s], slot[1].at[s], d))
        _stage_chains(chains)

    def mm_fn(step, slot, fin):
        chains, where = [], []
        for d in range(2):
            rows = _block_rows(step, d, n_blocks, blk_rows)
            for s in range(n_streams):
                col0 = s * heads_per_stream
                v_heads = [v_ref[col0 + h, rows, :] for h in range(heads_per_stream)]
                chains.append((slot[0].at[s], slot[1].at[s], d, v_heads, q_masks, mask_ref[d],
                               s_ref.at[s], d == 0))
                where.append((rows, col0))
        for outs, (rows, col0) in zip(_mm_chains(chains), where):
            _emit_block(outs, rows, col0, fin, acc_ref, g_ref, nw, o_ref)

    s_ref[...] = jnp.zeros_like(s_ref)
    _pipelined_steps(n_blocks, stage_fn, mm_fn, ((st_a, dec_a), (st_b, dec_b)))


def _mixer_call(name, arrays, col_units, heads_per_stream, n_streams, n_groups, dv, out_width, nw, seq_len):
    n = arrays[0].shape[1]
    br = _mixer_block_rows(seq_len)
    vs = n_streams * heads_per_stream
    slabs = (n_streams,) * 3 + (vs, vs) + (2 * n_streams,) * 2
    in_specs = [pl.BlockSpec((k, seq_len, LANES), lambda b, j, off=off: (off + j, b, 0))
                for k, off in zip(slabs, col_units)]
    in_specs += [pl.BlockSpec((1, dv), lambda b, j: (0, 0)), pl.BlockSpec((2, br, br), lambda b, j: (0, 0, 0))]
    slot = [pltpu.VMEM((n_streams, 4, 2, br, LANES), BF16),
            pltpu.VMEM((n_streams, 2, br // CHUNK * 8, LANES), F32)]
    return pl.pallas_call(
        functools.partial(_mixer_kernel, heads_per_stream),
        grid=(n // seq_len, n_groups),
        in_specs=in_specs,
        out_specs=pl.BlockSpec((vs, seq_len, dv), lambda b, j: (j, b, 0)),
        out_shape=jax.ShapeDtypeStruct((out_width // dv, n, dv), BF16),
        scratch_shapes=[pltpu.VMEM((vs, seq_len, dv), F32),
                        pltpu.VMEM((n_streams, 2, heads_per_stream, dv, LANES), F32)] + slot + slot,
        compiler_params=pltpu.CompilerParams(dimension_semantics=("arbitrary", "arbitrary"),
                                             vmem_limit_bytes=MIXER_VMEM_LIMIT),
        name=name,
    )(*arrays, nw, _block_masks(br))


def _mixer_block_rows(seq_len):
    return min(MIXER_BLOCK_ROWS, seq_len // 2)


def _mixer_streams(seq_len, bytes_per_row_per_stream, max_streams):
    return next(ns for ns in (4, 2, 1)
                if ns <= max_streams and (ns == 1 or ns * seq_len * bytes_per_row_per_stream <= MIXER_VMEM_BUDGET))


def _hg_mixer(pa, pk, plog, nw, seq_len):
    ns = _mixer_streams(seq_len, (5 * 2 + 2 * 4 + 2) * 2 * LANES + 4 * LANES, HG_HEADS)
    ng = HG_HEADS // ns
    units = (0, 0, HG_HEADS // ns, HG_HEADS // ns, 2 * HG_HEADS // ns, 0, HG_HEADS // ns)
    return _mixer_call("hg_mixer", (pa, pk, pk, pa, pa, plog, plog), units, 1, ns, ng, HG_DK, HG_WIDTH,
                       nw, seq_len)


def _gla_mixer(pa, plog, nw, seq_len):
    npair = GLA_HEADS // 2
    ns = _mixer_streams(seq_len, (3 * 2 + 2 * 4 + 2 * 4 + 4) * 2 * LANES + 8 * LANES, npair)
    ng = npair // ns
    units = (12 // ns, 14 // ns, 14 // ns, 8 // ns, 10 // ns, 2 * HG_HEADS // ns, (2 * HG_HEADS + npair) // ns)
    return _mixer_call("gla_mixer", (pa, pa, pa, pa, pa, plog, plog), units, 2, ns, ng, GLA_DV, GLA_WIDTH,
                       nw, seq_len)


def _out_proj_body(x_ref, mh_ref, mg_ref, mod_ref, w_ref, nw_ref):
    m = mod_ref[0]
    mixed = jnp.concatenate([mh_ref[j] for j in range(mh_ref.shape[0])]
                            + [mg_ref[j] for j in range(mg_ref.shape[0])], axis=1)
    y = _dot(mixed, w_ref[...])
    x = x_ref[...] + m[2:3, :] * y
    h = _rms(x, nw_ref[...]) * (1.0 + m[4:5, :]) + m[3:4, :]
    return x, h


def _out_proj_kernel(x_ref, mh_ref, mg_ref, mod_ref, w_ref, nw_ref, xo_ref, h_ref):
    x, h = _out_proj_body(x_ref, mh_ref, mg_ref, mod_ref, w_ref, nw_ref)
    xo_ref[...] = x
    h_ref[...] = h.astype(BF16)


def _out_proj_route_kernel(x_ref, mh_ref, mg_ref, mod_ref, w_ref, nw_ref, wr_ref, br_ref,
                           xo_ref, h_ref, route_ref):
    x, h = _out_proj_body(x_ref, mh_ref, mg_ref, mod_ref, w_ref, nw_ref)
    xo_ref[...] = x
    h_ref[...] = _pack_bf16_pair(h)
    wr_hi, wr_lo = _split_bf16(wr_ref[...])
    logits = _dot3(h, wr_hi, wr_lo) + br_ref[...]
    lane = lax.broadcasted_iota(jnp.int32, logits.shape, 1).astype(F32)
    neg = jnp.float32(-jnp.inf)
    logits = jnp.where(lane < N_EXPERTS, logits, neg)
    m1 = jnp.max(logits, axis=-1, keepdims=True)
    i1 = jnp.min(jnp.where(logits == m1, lane, float(LANES)), axis=-1, keepdims=True)
    rest = jnp.where(lane == i1, neg, logits)
    m2 = jnp.max(rest, axis=-1, keepdims=True)
    i2 = jnp.min(jnp.where(rest == m2, lane, float(LANES)), axis=-1, keepdims=True)
    e2 = jnp.exp(m2 - m1)
    den = 1.0 / (1.0 + e2)
    route = jnp.where(lane == 0.0, i1,
                      jnp.where(lane == 1.0, i2,
                                jnp.where(lane == 2.0, den, jnp.where(lane == 3.0, e2 * den, 0.0))))
    route_ref[...] = route


def _out_proj(x, mh, mg, mod, w, nw, seq_len, tm, router=None):
    n, d = x.shape
    tps = seq_len // tm
    in_specs = [
        pl.BlockSpec((tm, d), lambda i: (i, 0)),
        pl.BlockSpec((HG_WIDTH // LANES, tm, LANES), lambda i: (0, i, 0)),
        pl.BlockSpec((GLA_WIDTH // LANES, tm, LANES), lambda i: (0, i, 0)),
        pl.BlockSpec((1, 6, d), lambda i: (i // tps, 0, 0)),
        pl.BlockSpec((d, d), lambda i: (0, 0)),
        pl.BlockSpec((1, d), lambda i: (0, 0)),
    ]
    out_specs = [pl.BlockSpec((tm, d), lambda i: (i, 0)), pl.BlockSpec((tm, d), lambda i: (i, 0))]
    out_shape = [jax.ShapeDtypeStruct((n, d), F32), jax.ShapeDtypeStruct((n, d), BF16)]
    args = [x, mh, mg, mod, w, nw]
    body = _out_proj_kernel
    if router is not None:
        in_specs += [pl.BlockSpec((d, LANES), lambda i: (0, 0)), pl.BlockSpec((1, LANES), lambda i: (0, 0))]
        out_specs.append(pl.BlockSpec((tm, LANES), lambda i: (i, 0)))
        out_specs[1] = pl.BlockSpec((tm, d // 2), lambda i: (i, 0))
        out_shape[1] = jax.ShapeDtypeStruct((n, d // 2), jnp.uint32)
        out_shape.append(jax.ShapeDtypeStruct((n, LANES), F32))
        args += list(router)
        body = _out_proj_route_kernel
    return pl.pallas_call(
        body,
        grid=(n // tm,),
        in_specs=in_specs,
        out_specs=out_specs,
        out_shape=out_shape,
        compiler_params=_cparams("arbitrary"),
        name="out_proj",
    )(*args)


def _swiglu_step(xb, w1_ref, w3_ref, w2_ref, acc_ref):
    f = pl.program_id(1)
    a = _dot(xb, w1_ref[0])
    b = _dot(xb, w3_ref[0])
    part = _dot((a * _sigmoid(a) * b).astype(BF16), w2_ref[0])

    @pl.when(f == 0)
    def _():
        acc_ref[...] = part

    @pl.when(f > 0)
    def _():
        acc_ref[...] += part


def _dense_ffn_kernel(h_ref, x_ref, mod_ref, w1_ref, w3_ref, w2_ref, o_ref, acc_ref):
    _swiglu_step(h_ref[...], w1_ref, w3_ref, w2_ref, acc_ref)

    @pl.when(pl.program_id(1) == pl.num_programs(1) - 1)
    def _():
        o_ref[...] = x_ref[...] + mod_ref[0][5:6, :] * acc_ref[...]


def _dense_ffn(h, x, mod, w1, w3, w2, seq_len, tm, tf):
    n, d = x.shape
    ff = w1.shape[-1]
    tps = seq_len // tm
    return pl.pallas_call(
        _dense_ffn_kernel,
        grid=(n // tm, ff // tf),
        in_specs=[
            pl.BlockSpec((tm, d), lambda i, f: (i, 0)),
            pl.BlockSpec((tm, d), lambda i, f: (i, 0)),
            pl.BlockSpec((1, 6, d), lambda i, f: (i // tps, 0, 0)),
            pl.BlockSpec((1, d, tf), lambda i, f: (0, 0, f)),
            pl.BlockSpec((1, d, tf), lambda i, f: (0, 0, f)),
            pl.BlockSpec((1, tf, d), lambda i, f: (0, f, 0)),
        ],
        out_specs=pl.BlockSpec((tm, d), lambda i, f: (i, 0)),
        out_shape=jax.ShapeDtypeStruct((n, d), F32),
        scratch_shapes=[pltpu.VMEM((tm, d), F32)],
        compiler_params=_cparams("arbitrary", "arbitrary"),
        name="dense_ffn",
    )(h, x, mod, w1, w3, w2)


def _out_dense_kernel(x_ref, mh_ref, mg_ref, mod_ref, wo_ref, nw_ref, w1_ref, w3_ref, w2_ref, o_ref):
    x, h = _out_proj_body(x_ref, mh_ref, mg_ref, mod_ref, wo_ref, nw_ref)
    hb = h.astype(BF16)
    a = _dot(hb, w1_ref[0])
    b = _dot(hb, w3_ref[0])
    f = _dot((a * _sigmoid(a) * b).astype(BF16), w2_ref[0])
    o_ref[...] = x + mod_ref[0][5:6, :] * f


def _out_dense(x, mh, mg, mod, w_out, nw, w1, w3, w2, seq_len, tm):
    n, d = x.shape
    ff = w1.shape[-1]
    tps = seq_len // tm
    once = dict(pipeline_mode=pl.Buffered(1))
    return pl.pallas_call(
        _out_dense_kernel,
        grid=(n // tm,),
        in_specs=[
            pl.BlockSpec((tm, d), lambda i: (i, 0)),
            pl.BlockSpec((HG_WIDTH // LANES, tm, LANES), lambda i: (0, i, 0)),
            pl.BlockSpec((GLA_WIDTH // LANES, tm, LANES), lambda i: (0, i, 0)),
            pl.BlockSpec((1, 6, d), lambda i: (i // tps, 0, 0)),
            pl.BlockSpec((d, d), lambda i: (0, 0), **once),
            pl.BlockSpec((1, d), lambda i: (0, 0)),
            pl.BlockSpec((1, d, ff), lambda i: (0, 0, 0), **once),
            pl.BlockSpec((1, d, ff), lambda i: (0, 0, 0), **once),
            pl.BlockSpec((1, ff, d), lambda i: (0, 0, 0), **once),
        ],
        out_specs=pl.BlockSpec((tm, d), lambda i: (i, 0)),
        out_shape=jax.ShapeDtypeStruct((n, d), F32),
        compiler_params=pltpu.CompilerParams(dimension_semantics=("arbitrary",),
                                             vmem_limit_bytes=MIXER_VMEM_LIMIT),
        name="out_dense",
    )(x, mh, mg, mod, w_out, nw, w1, w3, w2)


def _expert_ffn_kernel(be_ref, nu_ref, nv_ref, xs_ref, w1_ref, w3_ref, w2_ref, o_ref, acc_ref):
    i = pl.program_id(0)
    last = pl.program_id(1) == pl.num_programs(1) - 1
    used = i < nu_ref[0]

    @pl.when(used)
    def _():
        row = lax.broadcasted_iota(jnp.int32, (xs_ref.shape[0], 1), 0)
        lo, hi = _unpack_bf16_pair(jnp.where(row < nv_ref[i], xs_ref[...], jnp.uint32(0)))
        xb = jnp.concatenate([lo.astype(BF16), hi.astype(BF16)], axis=1)
        _swiglu_step(xb, w1_ref, w3_ref, w2_ref, acc_ref)

    @pl.when(jnp.logical_and(used, last))
    def _():
        o_ref[...] = _pack_bf16_pair(acc_ref[...])

    @pl.when(jnp.logical_and(jnp.logical_not(used), last))
    def _():
        o_ref[...] = jnp.zeros_like(o_ref)


def _expert_ffn(xs, block_e, n_used, n_valid, w1, w3, w2, tm, tf):
    rows = xs.shape[0]
    d, ff = w1.shape[1:]
    nf = ff // tf

    def row_idx(i, f, be, nu, nv):
        return (jnp.minimum(i, nu[0] - 1), 0)

    def fsel(i, f, nu):
        return jnp.where(i < nu[0], f, nf - 1)

    wmode = dict(pipeline_mode=pl.Buffered(1)) if nf == 1 else {}
    grid_spec = pltpu.PrefetchScalarGridSpec(
        num_scalar_prefetch=3,
        grid=(rows // tm, nf),
        in_specs=[
            pl.BlockSpec((tm, d // 2), row_idx),
            pl.BlockSpec((1, d, tf), lambda i, f, be, nu, nv: (be[i], 0, fsel(i, f, nu)), **wmode),
            pl.BlockSpec((1, d, tf), lambda i, f, be, nu, nv: (be[i], 0, fsel(i, f, nu)), **wmode),
            pl.BlockSpec((1, tf, d), lambda i, f, be, nu, nv: (be[i], fsel(i, f, nu), 0), **wmode),
        ],
        out_specs=pl.BlockSpec((tm, d // 2), lambda i, f, be, nu, nv: (i, 0)),
        scratch_shapes=[pltpu.VMEM((tm, d), F32)],
    )
    return pl.pallas_call(
        _expert_ffn_kernel,
        grid_spec=grid_spec,
        out_shape=jax.ShapeDtypeStruct((rows, d // 2), jnp.uint32),
        compiler_params=_cparams("arbitrary", "arbitrary"),
        name="expert_ffn",
    )(block_e, n_used, n_valid, xs, w1, w3, w2)


def _scatter_rows(src, dest_a, dest_b, n_out):
    n, d = src.shape
    n_workers = SC_CORES * SC_SUBCORES
    per_worker = n // n_workers
    chunk = next(c for c in (32, 16, 8) if per_worker % c == 0)
    n_chunks = per_worker // chunk
    mesh = plsc.VectorSubcoreMesh(core_axis_name="c", subcore_axis_name="s")

    @functools.partial(
        pl.kernel, mesh=mesh, out_type=jax.ShapeDtypeStruct((n_out, d), src.dtype),
        scratch_types=[pltpu.VMEM((chunk,), jnp.int32), pltpu.VMEM((chunk,), jnp.int32),
                       pltpu.VMEM((chunk, d), src.dtype)])
    def scatter_kernel(src_hbm, da_hbm, db_hbm, out_hbm, ia_v, ib_v, rows_v):
        worker = lax.axis_index("s") * SC_CORES + lax.axis_index("c")
        base = worker * per_worker

        @pl.loop(0, n_chunks)
        def _(j):
            off = pl.multiple_of(base + j * chunk, chunk)
            pltpu.sync_copy(da_hbm.at[pl.ds(off, chunk)], ia_v)
            pltpu.sync_copy(db_hbm.at[pl.ds(off, chunk)], ib_v)
            pltpu.sync_copy(src_hbm.at[pl.ds(off, chunk)], rows_v)
            pltpu.sync_copy(rows_v, out_hbm.at[ia_v])
            pltpu.sync_copy(rows_v, out_hbm.at[ib_v])

    return scatter_kernel(src, dest_a, dest_b)


def _gather_rows(table, idx):
    n_rows = idx.shape[0]
    d = table.shape[1]
    n_workers = SC_CORES * SC_SUBCORES
    per_worker = n_rows // n_workers
    chunk = next(c for c in (64, 32, 16, 8) if per_worker % c == 0)
    n_chunks = per_worker // chunk
    mesh = plsc.VectorSubcoreMesh(core_axis_name="c", subcore_axis_name="s")

    @functools.partial(
        pl.kernel, mesh=mesh, out_type=jax.ShapeDtypeStruct((n_rows, d), table.dtype),
        scratch_types=[pltpu.VMEM((chunk,), jnp.int32), pltpu.VMEM((chunk, d), table.dtype),
                       pltpu.SemaphoreType.DMA])
    def gather_kernel(table_hbm, idx_hbm, out_hbm, idx_v, rows_v, sem):
        worker = lax.axis_index("s") * SC_CORES + lax.axis_index("c")
        base = worker * per_worker

        @pl.loop(0, n_chunks)
        def _(j):
            off = pl.multiple_of(base + j * chunk, chunk)
            pltpu.sync_copy(idx_hbm.at[pl.ds(off, chunk)], idx_v)
            pltpu.async_copy(table_hbm.at[idx_v], rows_v, sem).wait()
            pltpu.sync_copy(rows_v, out_hbm.at[pl.ds(off, chunk)])

    return gather_kernel(table, idx)


def _combine_body(x_ref, ya_ref, yb_ref, route_ref, mod_ref):
    r = route_ref[...]
    ga, gb = r[:, TOP_K:TOP_K + 1], r[:, TOP_K + 1:TOP_K + 2]
    ya = _unpack_bf16_pair(ya_ref[...])
    yb = _unpack_bf16_pair(yb_ref[...])
    f = jnp.concatenate([ga * ya[0] + gb * yb[0], ga * ya[1] + gb * yb[1]], axis=1)
    return x_ref[...] + mod_ref[0][5:6, :] * f


def _combine_kernel(x_ref, ya_ref, yb_ref, route_ref, mod_ref, o_ref):
    o_ref[...] = _combine_body(x_ref, ya_ref, yb_ref, route_ref, mod_ref)


def _combine_norm_kernel(x_ref, ya_ref, yb_ref, route_ref, mod_ref, nw_ref, o_ref):
    o_ref[...] = _rms(_combine_body(x_ref, ya_ref, yb_ref, route_ref, mod_ref), nw_ref[...])


def _combine(x, y2, route, mod, seq_len, tm, final_nw=None):
    n, d = x.shape
    tps = seq_len // tm
    nt = n // tm
    in_specs = [
        pl.BlockSpec((tm, d), lambda i: (i, 0)),
        pl.BlockSpec((tm, d // 2), lambda i: (i, 0)),
        pl.BlockSpec((tm, d // 2), lambda i: (i + nt, 0)),
        pl.BlockSpec((tm, LANES), lambda i: (i, 0)),
        pl.BlockSpec((1, 6, d), lambda i: (i // tps, 0, 0)),
    ]
    args = [x, y2, y2, route, mod]
    body = _combine_kernel
    if final_nw is not None:
        in_specs.append(pl.BlockSpec((1, d), lambda i: (0, 0)))
        args.append(final_nw)
        body = _combine_norm_kernel
    return pl.pallas_call(
        body,
        grid=(n // tm,),
        in_specs=in_specs,
        out_specs=pl.BlockSpec((tm, d), lambda i: (i, 0)),
        out_shape=jax.ShapeDtypeStruct((n, d), F32),
        compiler_params=_cparams("arbitrary"),
        name="moe_combine",
    )(*args)


def _final_norm_kernel(x_ref, w_ref, o_ref):
    o_ref[...] = _rms(x_ref[...], w_ref[...])


def _final_norm(x, w, tm):
    n, d = x.shape
    return pl.pallas_call(
        _final_norm_kernel,
        grid=(n // tm,),
        in_specs=[pl.BlockSpec((tm, d), lambda i: (i, 0)), pl.BlockSpec((1, d), lambda i: (0, 0))],
        out_specs=pl.BlockSpec((tm, d), lambda i: (i, 0)),
        out_shape=jax.ShapeDtypeStruct((n, d), F32),
        compiler_params=_cparams("arbitrary"),
        name="final_norm",
    )(x, w)


def _moe(h, x, mod, route, w1, w3, w2, seq_len, tm, tf, tm_c, final_nw):
    n, d = x.shape
    n_asg = n * TOP_K
    flat_e = route[:, :TOP_K].astype(jnp.int32).reshape(-1)
    onehot = (flat_e[:, None] == jnp.arange(N_EXPERTS, dtype=jnp.int32)[None, :]).astype(jnp.int32)
    csum = jnp.cumsum(onehot, axis=0)
    counts = csum[-1]
    rank = jnp.take_along_axis(csum, flat_e[:, None], axis=1)[:, 0] - 1
    pcounts = (counts + tm - 1) // tm * tm
    pend = jnp.cumsum(pcounts)
    pstart = pend - pcounts
    dest = pstart[flat_e] + rank
    n_blocks = -(-n_asg // tm) + N_EXPERTS
    block_start = jnp.arange(n_blocks, dtype=pend.dtype) * tm
    block_e = jnp.minimum(jnp.sum((pend[None, :] <= block_start[:, None]).astype(jnp.int32), axis=1),
                          N_EXPERTS - 1)
    n_used = (pend[-1:] // tm).astype(jnp.int32)
    n_valid = jnp.clip((pstart + counts)[block_e] - block_start, 0, tm).astype(jnp.int32)
    block_e = jnp.where(jnp.arange(n_blocks) < n_used[0], block_e, block_e[jnp.maximum(n_used[0] - 1, 0)])
    dest_ab = dest.reshape(n, TOP_K).T
    xs = _scatter_rows(h, dest_ab[0], dest_ab[1], n_blocks * tm)
    ys = _expert_ffn(xs, block_e, n_used, n_valid, w1, w3, w2, tm, tf)
    y2 = _gather_rows(ys, dest_ab.reshape(-1))
    return _combine(x, y2, route, mod, seq_len, tm_c, final_nw)


def _prep_weights(w_in, gla_w_gk_up, gla_b_gk, hg_lb_logits):
    depth = w_in.shape[0]
    sizes = (HG_WIDTH,) * 5 + (GLA_QK, GLA_QK, GLA_WIDTH, GLA_WIDTH, GLA_GATE_RANK, GLA_GATE_RANK)
    offs = [0]
    for s in sizes:
        offs.append(offs[-1] + s)
    seg = lambda k: w_in[:, :, offs[k]:offs[k + 1]]
    hq, hf_f, hf_b, hi, hg, gq, gk, gv, gg, lr_f, lr_b = [seg(k) for k in range(11)]
    pad = jnp.zeros(w_in.shape[:2] + (LANES - 2 * GLA_GATE_RANK,), w_in.dtype)
    w_perm = jnp.concatenate([hq, hi, hg, gq, gk, gv, gg, hf_f, hf_b, lr_f, lr_b, pad], axis=-1).astype(BF16)

    r = GLA_GATE_RANK
    wup = jnp.zeros((depth, LANES, GLA_HEADS // 2, 2, LANES), F32)
    bup = jnp.zeros((depth, GLA_HEADS // 2, 2, LANES), F32)
    for dd in range(2):
        wup = wup.at[:, dd * r:(dd + 1) * r, :, dd, :].set(
            gla_w_gk_up[:, dd].reshape(depth, r, GLA_HEADS // 2, LANES))
        bup = bup.at[:, :, dd, :].set(gla_b_gk[:, dd].reshape(depth, GLA_HEADS // 2, LANES))
    wup = wup.reshape(depth, LANES, GLA_HEADS * LANES).astype(BF16)
    bup = bup.reshape(depth, 1, GLA_HEADS * LANES)

    lb = jnp.cumsum(jax.nn.softmax(hg_lb_logits.astype(F32), axis=0), axis=0)
    lb = lb - lb[0]
    return w_perm, wup, bup, lb


def _pick(pref, total):
    t = min(pref, total)
    while total % t:
        t //= 2
    return t


def _trunk(x3, mods, weights):
    (norm1_w, w_perm, lb, wup, bup, hg_norm_w, gla_norm_w, w_out, norm2_w, w_ff1, w_ff3, w_ff2,
     w_router, b_router, w_e1, w_e3, w_e2, final_norm_w) = weights
    nb, seq_len, d = x3.shape
    depth = norm1_w.shape[0]
    n = nb * seq_len
    x = x3.reshape(n, d)
    tm = _pick(512, seq_len)
    tm_ffn = _pick(512, seq_len)
    tm_moe = _pick(512, n * TOP_K)
    for l in range(depth):
        mod = mods[l]
        pa, pk, plog = _in_proj(x, mod, norm1_w[l][None, :], w_perm[l], lb[l], wup[l], bup[l], seq_len,
                                _pick(1024, seq_len))
        mh = _hg_mixer(pa, pk, plog, hg_norm_w[l][None, :], seq_len)
        mg = _gla_mixer(pa, plog, gla_norm_w[l][None, :], seq_len)
        m = l // 2
        if l % 2 == 0:
            x = _out_dense(x, mh, mg, mod, w_out[l], norm2_w[l][None, :], w_ff1[m:m + 1], w_ff3[m:m + 1],
                           w_ff2[m:m + 1], seq_len, tm_ffn)
        else:
            wr = jnp.pad(w_router[m], ((0, 0), (0, LANES - N_EXPERTS)))
            br = jnp.pad(b_router[m], (0, LANES - N_EXPERTS))[None, :]
            x, h, route = _out_proj(x, mh, mg, mod, w_out[l], norm2_w[l][None, :], seq_len, tm,
                                    router=(wr, br))
            final_nw = final_norm_w[None, :] if l == depth - 1 else None
            x = _moe(h, x, mod, route, w_e1[m], w_e3[m], w_e2[m], seq_len, tm_moe,
                     w_e1.shape[-1], tm, final_nw)
    if depth % 2 == 1:
        x = _final_norm(x, final_norm_w[None, :], tm)
    return x.reshape(nb, seq_len, d)


def kernel(x_prompt, x_sample, c_prompt, c_sample, w_ada, b_ada, norm1_w, w_in, hg_lb_logits, gla_w_gk_up, gla_b_gk, hg_norm_w, gla_norm_w, w_out, norm2_w, w_ff1, w_ff3, w_ff2, w_router, b_router, w_e1, w_e3, w_e2, final_norm_w):
    depth, d = norm1_w.shape
    w_perm, wup, bup, lb = _prep_weights(w_in, gla_w_gk_up, gla_b_gk, hg_lb_logits)
    nbp = c_prompt.shape[0]
    mods = _ada_mods(jnp.concatenate([c_prompt, c_sample], axis=0), w_ada, b_ada)
    mods = mods.reshape(depth, mods.shape[1], 6, d)
    weights = (norm1_w, w_perm, lb, wup, bup, hg_norm_w, gla_norm_w, w_out.astype(BF16), norm2_w,
               w_ff1.astype(BF16), w_ff3.astype(BF16), w_ff2.astype(BF16), w_router, b_router,
               w_e1.astype(BF16), w_e3.astype(BF16), w_e2.astype(BF16), final_norm_w)
    y_prompt = _trunk(x_prompt, mods[:, :nbp], weights)
    y_sample = _trunk(x_sample, mods[:, nbp:], weights)
    return (y_prompt, y_sample)
```

```python
import functools

import jax
import jax.numpy as jnp
from jax import lax
from jax.experimental import pallas as pl
from jax.experimental.pallas import tpu as pltpu
from jax.experimental.pallas import tpu_sc as plsc

F32 = jnp.float32
BF16 = jnp.bfloat16

D_MODEL = 1024
HG_HEADS = 4
HG_DK = 128
HG_WIDTH = 512
GLA_HEADS = 4
GLA_DK = 64
GLA_DV = 128
GLA_QK = 256
GLA_WIDTH = 512
GLA_GATE_RANK = 16
GLA_GATE_NORMALIZER = 16.0
CHUNK = 64
N_EXPERTS = 8
TOP_K = 2
EPS = 1e-6
LANES = 128

PA_WIDTH = 3 * HG_WIDTH + 2 * GLA_QK + 2 * GLA_WIDTH
PG_WIDTH = 2 * HG_WIDTH + LANES
PK_WIDTH = 2 * HG_WIDTH
PL_WIDTH = (2 * HG_HEADS + 2 * (GLA_HEADS // 2)) * 2 * LANES

VMEM_LIMIT = 48 * 1024 * 1024
SC_CORES = 2
SC_SUBCORES = 16
MIXER_BLOCK_ROWS = 128
MIXER_VMEM_LIMIT = 56 * 1024 * 1024
MIXER_VMEM_BUDGET = 46 * 1024 * 1024


def _cparams(*sem):
    return pltpu.CompilerParams(dimension_semantics=sem, vmem_limit_bytes=VMEM_LIMIT)


def _sigmoid(x):
    return 1.0 / (1.0 + jnp.exp(-x))


def _split_bf16(a):
    hi = a.astype(BF16)
    lo = (a - hi.astype(F32)).astype(BF16)
    return hi, lo


def _dot(a, b):
    return jnp.dot(a, b, preferred_element_type=F32)


def _dot_nt(a, b):
    return lax.dot_general(a, b, (((1,), (1,)), ((), ())), preferred_element_type=F32)


def _dot_tn(a, b):
    return lax.dot_general(a, b, (((0,), (0,)), ((), ())), preferred_element_type=F32)


def _dot3(a, b_hi, b_lo):
    a_hi, a_lo = _split_bf16(a)
    return _dot(a_hi, b_hi) + (_dot(a_hi, b_lo) + _dot(a_lo, b_hi))


def _pack_bf16_pair(v):
    w = v.shape[1] // 2
    lo = lax.bitcast_convert_type(v[:, :w].astype(BF16).astype(F32), jnp.uint32)
    hi = lax.bitcast_convert_type(v[:, w:].astype(BF16).astype(F32), jnp.uint32)
    return (lo >> 16) | hi


def _unpack_bf16_pair(u):
    lo = lax.bitcast_convert_type(u << 16, F32)
    hi = lax.bitcast_convert_type(u & jnp.uint32(0xFFFF0000), F32)
    return lo, hi


def _rms(x, w):
    ms = jnp.mean(x * x, axis=-1, keepdims=True)
    return x * lax.rsqrt(ms + EPS) * w


def _ada_kernel(c_ref, w_ref, b_ref, o_ref):
    c = c_ref[...]
    s = c * _sigmoid(c)
    w_hi, w_lo = _split_bf16(w_ref[0])
    o_ref[0] = _dot3(s, w_hi, w_lo) + b_ref[0]


def _ada_mods(c, w_ada, b_ada):
    depth, d, e = w_ada.shape
    nb = c.shape[0]
    tn = 1536
    return pl.pallas_call(
        _ada_kernel,
        grid=(depth, e // tn),
        in_specs=[
            pl.BlockSpec((nb, d), lambda l, j: (0, 0)),
            pl.BlockSpec((1, d, tn), lambda l, j: (l, 0, j)),
            pl.BlockSpec((1, 1, tn), lambda l, j: (l, 0, j)),
        ],
        out_specs=pl.BlockSpec((1, nb, tn), lambda l, j: (l, 0, j)),
        out_shape=jax.ShapeDtypeStruct((depth, nb, e), F32),
        compiler_params=_cparams("arbitrary", "arbitrary"),
        name="ada_mods",
    )(c, w_ada, b_ada.reshape(depth, 1, e))


def _hg_gate(z, lb):
    e = jnp.exp(-jnp.abs(z))
    r = 1.0 / (1.0 + e)
    er = e * r
    pos = z >= 0.0
    sig = jnp.where(pos, r, er)
    sig_neg = jnp.where(pos, er, r)
    oml = 1.0 - lb
    return oml * sig_neg, jnp.log(lb + oml * sig)


def _store_slabs(ref, slab0, val):
    for j in range(val.shape[1] // LANES):
        ref[slab0 + j] = val[:, j * LANES:(j + 1) * LANES].astype(ref.dtype)


def _store_hi_lo(pl_ref, blk, logg):
    hi, lo = _split_bf16(logg)
    pl_ref[2 * blk] = hi
    pl_ref[2 * blk + 1] = lo


def _in_proj_kernel(x_ref, mod_ref, nw_ref, w_ref, lb_ref, wup_ref, bup_ref, pa_ref, pk_ref, pl_ref):
    m = mod_ref[0]
    h = _rms(x_ref[...], nw_ref[...]) * (1.0 + m[1:2, :]) + m[0:1, :]
    hb = h.astype(BF16)
    proj = lambda lo, hi: _dot(hb, w_ref[:, lo:hi])

    def plain(c):
        _store_slabs(pa_ref, 4 * c, proj(c * 512, (c + 1) * 512))

    lb = lb_ref[...]
    for d in range(2):
        z = proj(PA_WIDTH + d * HG_WIDTH, PA_WIDTH + (d + 1) * HG_WIDTH)
        k, logf = _hg_gate(z, lb[d:d + 1, :])
        _store_slabs(pk_ref, d * HG_HEADS, k)
        for hh in range(HG_HEADS):
            _store_hi_lo(pl_ref, d * HG_HEADS + hh, logf[:, hh * HG_DK:(hh + 1) * HG_DK])
        plain(1 + d)
    lr = proj(PA_WIDTH + 2 * HG_WIDTH, PA_WIDTH + PG_WIDTH).astype(BF16)
    zg = _dot(lr, wup_ref[...]) + bup_ref[...]
    logg = (jnp.minimum(zg, 0.0) - jnp.log(1.0 + jnp.exp(-jnp.abs(zg)))) * (1.0 / GLA_GATE_NORMALIZER)
    npair = GLA_HEADS // 2
    for p in range(npair):
        for d in range(2):
            src = (p * 2 + d) * LANES
            _store_hi_lo(pl_ref, 2 * HG_HEADS + d * npair + p, logg[:, src:src + LANES])
    plain(4)
    q = proj(0, HG_WIDTH)
    _store_slabs(pa_ref, 0, q * _sigmoid(q) * (HG_DK ** -0.5))
    plain(5)
    qk = proj(3 * 512, 4 * 512)
    _store_slabs(pa_ref, 12, qk[:, :GLA_QK] * (GLA_DK ** -0.5))
    _store_slabs(pa_ref, 12 + GLA_QK // LANES, qk[:, GLA_QK:])


def _in_proj(x, mod, nw, w_all, layer, lb, wup, bup, seq_len, tm):
    n, d = x.shape
    tps = seq_len // tm
    const = lambda shape: pl.BlockSpec(shape, lambda i: (0, 0))
    return pl.pallas_call(
        _in_proj_kernel,
        grid=(n // tm,),
        in_specs=[
            pl.BlockSpec((tm, d), lambda i: (i, 0)),
            pl.BlockSpec((1, 6, d), lambda i: (i // tps, 0, 0)),
            const((1, d)),
            pl.BlockSpec((None, d, PA_WIDTH + PG_WIDTH), lambda i: (layer, 0, 0)),
            const((2, HG_WIDTH)),
            const((LANES, GLA_HEADS * LANES)),
            const((1, GLA_HEADS * LANES)),
        ],
        out_specs=[pl.BlockSpec((w // LANES, tm, LANES), lambda i: (0, i, 0))
                   for w in (PA_WIDTH, PK_WIDTH, PL_WIDTH)],
        out_shape=[jax.ShapeDtypeStruct((w // LANES, n, LANES), BF16) for w in (PA_WIDTH, PK_WIDTH, PL_WIDTH)],
        compiler_params=_cparams("arbitrary"),
        name="in_proj",
    )(x, mod, nw, w_all, lb, wup, bup)


def _block_masks(rows):
    r = jnp.arange(rows, dtype=jnp.int32)[:, None]
    c = jnp.arange(rows, dtype=jnp.int32)[None, :]
    same = (r // CHUNK) == (c // CHUNK)
    return jnp.stack([same & (c <= r), same & (c >= r)]).astype(F32)


def _chunk_slices(a, n_sub):
    return [a[c * CHUNK:(c + 1) * CHUNK] for c in range(n_sub)]


def _decay_factors(cs, fwd):
    n_sub = cs.shape[0] // CHUNK
    w = cs.shape[1] // 2
    b = cs[:, :w] + cs[:, w:]
    r = CHUNK // 2 if fwd else CHUNK // 2 - 1
    last = CHUNK - 1 if fwd else 0
    out = []
    for bc in _chunk_slices(b, n_sub):
        b_ref = bc[r:r + 1, :]
        b_last = bc[last:last + 1, :]
        out.append((jnp.exp(bc - b_ref), jnp.exp(b_ref - bc), jnp.exp(b_ref), jnp.exp(b_last - b_ref),
                    jnp.exp(b_last)))
    return out


def _stage_chains(chains):
    sums = [_dot(tri, hl) for _, _, hl, tri, _, _, _, _ in chains]
    for cs, (q_fn, k_fn, _, _, fwd, st_ref, dec_ref, d) in zip(sums, chains):
        fac = _decay_factors(cs, fwd)
        q = q_fn()
        k = k_fn()
        for c in range(len(fac)):
            sl = slice(c * CHUNK, (c + 1) * CHUNK)
            a, ainv, e_ref, e_last_ref, decay = fac[c]
            qa = q[sl] * a
            ka = k[sl] * ainv
            st_ref[0, d, sl, :] = qa.astype(BF16)
            st_ref[1, d, sl, :] = ka.astype(BF16)
            st_ref[2, d, sl, :] = (qa * e_ref).astype(BF16)
            st_ref[3, d, sl, :] = (ka * e_last_ref).astype(BF16)
            dec_ref[d, c * 8:(c + 1) * 8, :] = jnp.broadcast_to(decay, (8, decay.shape[1]))


def _mm_chains(chains):
    n_sub = chains[0][0].shape[2] // CHUNK
    heads = []
    for ci, (st_ref, dec_ref, d, v_heads, q_masks, maskf, s_ref, fwd) in enumerate(chains):
        for h, v in enumerate(v_heads):
            hm = q_masks[h]
            sel = (lambda a: a) if hm is None else (lambda a, hm=hm: jnp.where(hm, a, jnp.zeros_like(a)))
            heads.append(dict(ci=ci, st=st_ref, dec=dec_ref, d=d, h=h, v=v, vs=_chunk_slices(v, n_sub),
                              sel=sel, maskf=maskf, s_ref=s_ref, fwd=fwd))
    for e in heads:
        e["sc"] = _dot_nt(e["sel"](e["st"][0, e["d"]]), e["st"][1, e["d"]])
    for e in heads:
        e["ds"] = [_dot_tn(e["vs"][c], e["st"][3, e["d"], c * CHUNK:(c + 1) * CHUNK, :]) for c in range(n_sub)]
    for e in heads:
        p = jnp.where(e["maskf"] > 0.5, e["sc"], 0.0).astype(BF16)
        e["oi"] = _chunk_slices(_dot(p, e["v"]), n_sub)
    outs = [[] for _ in chains]
    for e in heads:
        d, st_ref, dec_ref = e["d"], e["st"], e["dec"]
        s_t = e["s_ref"][d, e["h"]]
        o = [None] * n_sub
        for c in (range(n_sub) if e["fwd"] else range(n_sub - 1, -1, -1)):
            sl = slice(c * CHUNK, (c + 1) * CHUNK)
            o[c] = e["oi"][c] + _dot_nt(e["sel"](st_ref[2, d, sl, :]), s_t.astype(BF16))
            s_t = dec_ref[d, c * 8:c * 8 + 1, :] * s_t + e["ds"][c]
        e["s_ref"][d, e["h"]] = s_t
        outs[e["ci"]].append(jnp.concatenate(o, axis=0))
    return outs


def _pipelined_steps(n_blocks, stage_fn, mm_fn, slots):
    n_pairs = n_blocks // 2

    def pair(p, fin0, fin1, last):
        stage_fn(2 * p + 1, slots[1])
        mm_fn(2 * p, slots[0], fin0)
        if not last:
            stage_fn(2 * p + 2, slots[0])
        mm_fn(2 * p + 1, slots[1], fin1)

    groups = []
    for p in range(n_pairs):
        key = (2 * p >= n_pairs, 2 * p + 1 >= n_pairs, p == n_pairs - 1)
        if groups and groups[-1][2] == key:
            groups[-1][1] = p + 1
        else:
            groups.append([p, p + 1, key])

    stage_fn(0, slots[0])
    for lo, hi, key in groups:
        if hi - lo == 1:
            pair(lo, *key)
        else:
            def body(p, carry, key=key):
                pair(p, *key)
                return carry
            lax.fori_loop(lo, hi, body, 0)


def _block_rows(step, d, n_blocks, blk_rows):
    blk = step if d == 0 else n_blocks - 1 - step
    if isinstance(blk, int):
        return pl.ds(blk * blk_rows, blk_rows)
    return pl.ds(pl.multiple_of(blk * blk_rows, blk_rows), blk_rows)


def _emit_block(outs, rows, col0, fin, acc_ref, g_ref, nw, o_ref):
    for h, o in enumerate(outs):
        if fin:
            g = g_ref[col0 + h, rows, :].astype(F32)
            o_ref[col0 + h, rows, :] = (_rms(o + acc_ref[col0 + h, rows, :], nw)
                                        * (g * _sigmoid(g))).astype(o_ref.dtype)
        else:
            acc_ref[col0 + h, rows, :] = o


def _mixer_kernel(heads_per_stream, q_ref, kf_ref, kb_ref, v_ref, g_ref, hlf_ref, hlb_ref, nw_ref, mask_ref,
                  o_ref, acc_ref, s_ref, st_a, dec_a, st_b, dec_b):
    blk_rows = mask_ref.shape[1]
    n_blocks = q_ref.shape[1] // blk_rows
    n_streams = q_ref.shape[0]
    nw = nw_ref[...]
    k_refs = (kf_ref, kb_ref)
    hl_refs = (hlf_ref, hlb_ref)
    if heads_per_stream == 1:
        q_masks = [None]
    else:
        lane = lax.broadcasted_iota(jnp.int32, (1, LANES), 1)
        q_masks = [lane < GLA_DK, lane >= GLA_DK]

    def stage_fn(step, slot):
        chains = []
        for d in range(2):
            tri = mask_ref[d].astype(BF16)
            rows = _block_rows(step, d, n_blocks, blk_rows)
            for s in range(n_streams):
                hl = jnp.concatenate([hl_refs[d][2 * s, rows, :], hl_refs[d][2 * s + 1, rows, :]], axis=1)
                chains.append((lambda rows=rows, s=s: q_ref[s, rows, :].astype(F32),
                               lambda rows=rows, s=s, d=d: k_refs[d][s, rows, :].astype(F32),
                               hl, tri, d == 0, slot[0].at[s], slot[1].at[s], d))
        _stage_chains(chains)

    def mm_fn(step, slot, fin):
        chains, where = [], []
        for d in range(2):
            rows = _block_rows(step, d, n_blocks, blk_rows)
            for s in range(n_streams):
                col0 = s * heads_per_stream
                v_heads = [v_ref[col0 + h, rows, :] for h in range(heads_per_stream)]
                chains.append((slot[0].at[s], slot[1].at[s], d, v_heads, q_masks, mask_ref[d],
                               s_ref.at[s], d == 0))
                where.append((rows, col0))
        for outs, (rows, col0) in zip(_mm_chains(chains), where):
            _emit_block(outs, rows, col0, fin, acc_ref, g_ref, nw, o_ref)

    s_ref[...] = jnp.zeros_like(s_ref)
    _pipelined_steps(n_blocks, stage_fn, mm_fn, ((st_a, dec_a), (st_b, dec_b)))


def _mixer_call(name, arrays, col_units, heads_per_stream, n_streams, n_groups, dv, out_width, nw, seq_len):
    n = arrays[0].shape[1]
    br = _mixer_block_rows(seq_len)
    vs = n_streams * heads_per_stream
    slabs = (n_streams,) * 3 + (vs, vs) + (2 * n_streams,) * 2
    in_specs = [pl.BlockSpec((k, seq_len, LANES), lambda b, j, off=off: (off + j, b, 0))
                for k, off in zip(slabs, col_units)]
    in_specs += [pl.BlockSpec((1, dv), lambda b, j: (0, 0)), pl.BlockSpec((2, br, br), lambda b, j: (0, 0, 0))]
    slot = [pltpu.VMEM((n_streams, 4, 2, br, LANES), BF16),
            pltpu.VMEM((n_streams, 2, br // CHUNK * 8, LANES), F32)]
    return pl.pallas_call(
        functools.partial(_mixer_kernel, heads_per_stream),
        grid=(n // seq_len, n_groups),
        in_specs=in_specs,
        out_specs=pl.BlockSpec((vs, seq_len, dv), lambda b, j: (j, b, 0)),
        out_shape=jax.ShapeDtypeStruct((out_width // dv, n, dv), BF16),
        scratch_shapes=[pltpu.VMEM((vs, seq_len, dv), F32),
                        pltpu.VMEM((n_streams, 2, heads_per_stream, dv, LANES), F32)] + slot + slot,
        compiler_params=pltpu.CompilerParams(dimension_semantics=("arbitrary", "arbitrary"),
                                             vmem_limit_bytes=MIXER_VMEM_LIMIT),
        name=name,
    )(*arrays, nw, _block_masks(br))


def _mixer_block_rows(seq_len):
    return min(MIXER_BLOCK_ROWS, seq_len // 2)


def _mixer_streams(seq_len, bytes_per_row_per_stream, max_streams):
    return next(ns for ns in (4, 2, 1)
                if ns <= max_streams and (ns == 1 or ns * seq_len * bytes_per_row_per_stream <= MIXER_VMEM_BUDGET))


def _hg_mixer(pa, pk, plog, nw, seq_len):
    ns = _mixer_streams(seq_len, (5 * 2 + 2 * 4 + 2) * 2 * LANES + 4 * LANES, HG_HEADS)
    ng = HG_HEADS // ns
    units = (0, 0, HG_HEADS // ns, HG_HEADS // ns, 2 * HG_HEADS // ns, 0, HG_HEADS // ns)
    return _mixer_call("hg_mixer", (pa, pk, pk, pa, pa, plog, plog), units, 1, ns, ng, HG_DK, HG_WIDTH,
                       nw, seq_len)


def _gla_mixer(pa, plog, nw, seq_len):
    npair = GLA_HEADS // 2
    ns = _mixer_streams(seq_len, (3 * 2 + 2 * 4 + 2 * 4 + 4) * 2 * LANES + 8 * LANES, npair)
    ng = npair // ns
    units = (12 // ns, 14 // ns, 14 // ns, 8 // ns, 10 // ns, 2 * HG_HEADS // ns, (2 * HG_HEADS + npair) // ns)
    return _mixer_call("gla_mixer", (pa, pa, pa, pa, pa, plog, plog), units, 2, ns, ng, GLA_DV, GLA_WIDTH,
                       nw, seq_len)


def _out_proj_body(x_ref, mh_ref, mg_ref, mod_ref, w_ref, nw_ref):
    m = mod_ref[0]
    mixed = jnp.concatenate([mh_ref[j] for j in range(mh_ref.shape[0])]
                            + [mg_ref[j] for j in range(mg_ref.shape[0])], axis=1)
    y = _dot(mixed, w_ref[...])
    x = x_ref[...] + m[2:3, :] * y
    h = _rms(x, nw_ref[...]) * (1.0 + m[4:5, :]) + m[3:4, :]
    return x, h


def _out_proj_kernel(x_ref, mh_ref, mg_ref, mod_ref, w_ref, nw_ref, xo_ref, h_ref):
    x, h = _out_proj_body(x_ref, mh_ref, mg_ref, mod_ref, w_ref, nw_ref)
    xo_ref[...] = x
    h_ref[...] = h.astype(BF16)


def _out_proj_route_kernel(x_ref, mh_ref, mg_ref, mod_ref, w_ref, nw_ref, wr_ref, br_ref,
                           xo_ref, h_ref, route_ref):
    x, h = _out_proj_body(x_ref, mh_ref, mg_ref, mod_ref, w_ref, nw_ref)
    xo_ref[...] = x
    h_ref[...] = _pack_bf16_pair(h)
    wr_hi, wr_lo = _split_bf16(wr_ref[...])
    logits = _dot3(h, wr_hi, wr_lo) + br_ref[...]
    lane = lax.broadcasted_iota(jnp.int32, logits.shape, 1).astype(F32)
    neg = jnp.float32(-jnp.inf)
    logits = jnp.where(lane < N_EXPERTS, logits, neg)
    m1 = jnp.max(logits, axis=-1, keepdims=True)
    i1 = jnp.min(jnp.where(logits == m1, lane, float(LANES)), axis=-1, keepdims=True)
    rest = jnp.where(lane == i1, neg, logits)
    m2 = jnp.max(rest, axis=-1, keepdims=True)
    i2 = jnp.min(jnp.where(rest == m2, lane, float(LANES)), axis=-1, keepdims=True)
    e2 = jnp.exp(m2 - m1)
    den = 1.0 / (1.0 + e2)
    route = jnp.where(lane == 0.0, i1,
                      jnp.where(lane == 1.0, i2,
                                jnp.where(lane == 2.0, den, jnp.where(lane == 3.0, e2 * den, 0.0))))
    route_ref[...] = route


def _out_proj(x, mh, mg, mod, w_all, layer, nw, seq_len, tm, router=None):
    n, d = x.shape
    tps = seq_len // tm
    in_specs = [
        pl.BlockSpec((tm, d), lambda i: (i, 0)),
        pl.BlockSpec((HG_WIDTH // LANES, tm, LANES), lambda i: (0, i, 0)),
        pl.BlockSpec((GLA_WIDTH // LANES, tm, LANES), lambda i: (0, i, 0)),
        pl.BlockSpec((1, 6, d), lambda i: (i // tps, 0, 0)),
        pl.BlockSpec((None, d, d), lambda i: (layer, 0, 0)),
        pl.BlockSpec((1, d), lambda i: (0, 0)),
    ]
    out_specs = [pl.BlockSpec((tm, d), lambda i: (i, 0)), pl.BlockSpec((tm, d), lambda i: (i, 0))]
    out_shape = [jax.ShapeDtypeStruct((n, d), F32), jax.ShapeDtypeStruct((n, d), BF16)]
    args = [x, mh, mg, mod, w_all, nw]
    body = _out_proj_kernel
    if router is not None:
        in_specs += [pl.BlockSpec((d, LANES), lambda i: (0, 0)), pl.BlockSpec((1, LANES), lambda i: (0, 0))]
        out_specs.append(pl.BlockSpec((tm, LANES), lambda i: (i, 0)))
        out_specs[1] = pl.BlockSpec((tm, d // 2), lambda i: (i, 0))
        out_shape[1] = jax.ShapeDtypeStruct((n, d // 2), jnp.uint32)
        out_shape.append(jax.ShapeDtypeStruct((n, LANES), F32))
        args += list(router)
        body = _out_proj_route_kernel
    return pl.pallas_call(
        body,
        grid=(n // tm,),
        in_specs=in_specs,
        out_specs=out_specs,
        out_shape=out_shape,
        compiler_params=_cparams("arbitrary"),
        name="out_proj",
    )(*args)


def _swiglu_step(xb, w1_ref, w3_ref, w2_ref, acc_ref):
    f = pl.program_id(1)
    a = _dot(xb, w1_ref[0])
    b = _dot(xb, w3_ref[0])
    part = _dot((a * _sigmoid(a) * b).astype(BF16), w2_ref[0])

    @pl.when(f == 0)
    def _():
        acc_ref[...] = part

    @pl.when(f > 0)
    def _():
        acc_ref[...] += part


def _dense_ffn_kernel(h_ref, x_ref, mod_ref, w1_ref, w3_ref, w2_ref, o_ref, acc_ref):
    _swiglu_step(h_ref[...], w1_ref, w3_ref, w2_ref, acc_ref)

    @pl.when(pl.program_id(1) == pl.num_programs(1) - 1)
    def _():
        o_ref[...] = x_ref[...] + mod_ref[0][5:6, :] * acc_ref[...]


def _dense_ffn(h, x, mod, w1, w3, w2, seq_len, tm, tf):
    n, d = x.shape
    ff = w1.shape[-1]
    tps = seq_len // tm
    return pl.pallas_call(
        _dense_ffn_kernel,
        grid=(n // tm, ff // tf),
        in_specs=[
            pl.BlockSpec((tm, d), lambda i, f: (i, 0)),
            pl.BlockSpec((tm, d), lambda i, f: (i, 0)),
            pl.BlockSpec((1, 6, d), lambda i, f: (i // tps, 0, 0)),
            pl.BlockSpec((1, d, tf), lambda i, f: (0, 0, f)),
            pl.BlockSpec((1, d, tf), lambda i, f: (0, 0, f)),
            pl.BlockSpec((1, tf, d), lambda i, f: (0, f, 0)),
        ],
        out_specs=pl.BlockSpec((tm, d), lambda i, f: (i, 0)),
        out_shape=jax.ShapeDtypeStruct((n, d), F32),
        scratch_shapes=[pltpu.VMEM((tm, d), F32)],
        compiler_params=_cparams("arbitrary", "arbitrary"),
        name="dense_ffn",
    )(h, x, mod, w1, w3, w2)


def _out_dense_kernel(x_ref, mh_ref, mg_ref, mod_ref, wo_ref, nw_ref, w1_ref, w3_ref, w2_ref, o_ref):
    x, h = _out_proj_body(x_ref, mh_ref, mg_ref, mod_ref, wo_ref, nw_ref)
    hb = h.astype(BF16)
    a = _dot(hb, w1_ref[0])
    b = _dot(hb, w3_ref[0])
    f = _dot((a * _sigmoid(a) * b).astype(BF16), w2_ref[0])
    o_ref[...] = x + mod_ref[0][5:6, :] * f


def _out_dense(x, mh, mg, mod, w_out, layer, nw, w1, w3, w2, m, seq_len, tm):
    n, d = x.shape
    ff = w1.shape[-1]
    tps = seq_len // tm
    once = dict(pipeline_mode=pl.Buffered(1))
    return pl.pallas_call(
        _out_dense_kernel,
        grid=(n // tm,),
        in_specs=[
            pl.BlockSpec((tm, d), lambda i: (i, 0)),
            pl.BlockSpec((HG_WIDTH // LANES, tm, LANES), lambda i: (0, i, 0)),
            pl.BlockSpec((GLA_WIDTH // LANES, tm, LANES), lambda i: (0, i, 0)),
            pl.BlockSpec((1, 6, d), lambda i: (i // tps, 0, 0)),
            pl.BlockSpec((None, d, d), lambda i: (layer, 0, 0), **once),
            pl.BlockSpec((1, d), lambda i: (0, 0)),
            pl.BlockSpec((1, d, ff), lambda i: (m, 0, 0), **once),
            pl.BlockSpec((1, d, ff), lambda i: (m, 0, 0), **once),
            pl.BlockSpec((1, ff, d), lambda i: (m, 0, 0), **once),
        ],
        out_specs=pl.BlockSpec((tm, d), lambda i: (i, 0)),
        out_shape=jax.ShapeDtypeStruct((n, d), F32),
        compiler_params=pltpu.CompilerParams(dimension_semantics=("arbitrary",),
                                             vmem_limit_bytes=MIXER_VMEM_LIMIT),
        name="out_dense",
    )(x, mh, mg, mod, w_out, nw, w1, w3, w2)


def _expert_ffn_kernel(be_ref, nu_ref, nv_ref, xs_ref, w1_ref, w3_ref, w2_ref, o_ref, acc_ref):
    i = pl.program_id(0)
    last = pl.program_id(1) == pl.num_programs(1) - 1
    used = i < nu_ref[0]

    @pl.when(used)
    def _():
        row = lax.broadcasted_iota(jnp.int32, (xs_ref.shape[0], 1), 0)
        lo, hi = _unpack_bf16_pair(jnp.where(row < nv_ref[i], xs_ref[...], jnp.uint32(0)))
        xb = jnp.concatenate([lo.astype(BF16), hi.astype(BF16)], axis=1)
        _swiglu_step(xb, w1_ref, w3_ref, w2_ref, acc_ref)

    @pl.when(jnp.logical_and(used, last))
    def _():
        o_ref[...] = _pack_bf16_pair(acc_ref[...])

    @pl.when(jnp.logical_and(jnp.logical_not(used), last))
    def _():
        o_ref[...] = jnp.zeros_like(o_ref)


def _expert_ffn(xs, block_e, n_used, n_valid, w1, w3, w2, tm, tf):
    rows = xs.shape[0]
    d, ff = w1.shape[1:]
    nf = ff // tf

    def row_idx(i, f, be, nu, nv):
        return (jnp.minimum(i, nu[0] - 1), 0)

    def fsel(i, f, nu):
        return jnp.where(i < nu[0], f, nf - 1)

    wmode = dict(pipeline_mode=pl.Buffered(1)) if nf == 1 else {}
    grid_spec = pltpu.PrefetchScalarGridSpec(
        num_scalar_prefetch=3,
        grid=(rows // tm, nf),
        in_specs=[
            pl.BlockSpec((tm, d // 2), row_idx),
            pl.BlockSpec((1, d, tf), lambda i, f, be, nu, nv: (be[i], 0, fsel(i, f, nu)), **wmode),
            pl.BlockSpec((1, d, tf), lambda i, f, be, nu, nv: (be[i], 0, fsel(i, f, nu)), **wmode),
            pl.BlockSpec((1, tf, d), lambda i, f, be, nu, nv: (be[i], fsel(i, f, nu), 0), **wmode),
        ],
        out_specs=pl.BlockSpec((tm, d // 2), lambda i, f, be, nu, nv: (i, 0)),
        scratch_shapes=[pltpu.VMEM((tm, d), F32)],
    )
    return pl.pallas_call(
        _expert_ffn_kernel,
        grid_spec=grid_spec,
        out_shape=jax.ShapeDtypeStruct((rows, d // 2), jnp.uint32),
        compiler_params=_cparams("arbitrary", "arbitrary"),
        name="expert_ffn",
    )(block_e, n_used, n_valid, xs, w1, w3, w2)


def _scatter_rows(src, dest_a, dest_b, n_out):
    n, d = src.shape
    n_workers = SC_CORES * SC_SUBCORES
    per_worker = n // n_workers
    chunk = next(c for c in (32, 16, 8) if per_worker % c == 0)
    n_chunks = per_worker // chunk
    mesh = plsc.VectorSubcoreMesh(core_axis_name="c", subcore_axis_name="s")

    @functools.partial(
        pl.kernel, mesh=mesh, out_type=jax.ShapeDtypeStruct((n_out, d), src.dtype),
        scratch_types=[pltpu.VMEM((chunk,), jnp.int32), pltpu.VMEM((chunk,), jnp.int32),
                       pltpu.VMEM((chunk, d), src.dtype)])
    def scatter_kernel(src_hbm, da_hbm, db_hbm, out_hbm, ia_v, ib_v, rows_v):
        worker = lax.axis_index("s") * SC_CORES + lax.axis_index("c")
        base = worker * per_worker

        @pl.loop(0, n_chunks)
        def _(j):
            off = pl.multiple_of(base + j * chunk, chunk)
            pltpu.sync_copy(da_hbm.at[pl.ds(off, chunk)], ia_v)
            pltpu.sync_copy(db_hbm.at[pl.ds(off, chunk)], ib_v)
            pltpu.sync_copy(src_hbm.at[pl.ds(off, chunk)], rows_v)
            pltpu.sync_copy(rows_v, out_hbm.at[ia_v])
            pltpu.sync_copy(rows_v, out_hbm.at[ib_v])

    return scatter_kernel(src, dest_a, dest_b)


def _gather_rows(table, idx):
    n_rows = idx.shape[0]
    d = table.shape[1]
    n_workers = SC_CORES * SC_SUBCORES
    per_worker = n_rows // n_workers
    chunk = next(c for c in (64, 32, 16, 8) if per_worker % c == 0)
    n_chunks = per_worker // chunk
    mesh = plsc.VectorSubcoreMesh(core_axis_name="c", subcore_axis_name="s")

    @functools.partial(
        pl.kernel, mesh=mesh, out_type=jax.ShapeDtypeStruct((n_rows, d), table.dtype),
        scratch_types=[pltpu.VMEM((chunk,), jnp.int32), pltpu.VMEM((chunk, d), table.dtype),
                       pltpu.SemaphoreType.DMA])
    def gather_kernel(table_hbm, idx_hbm, out_hbm, idx_v, rows_v, sem):
        worker = lax.axis_index("s") * SC_CORES + lax.axis_index("c")
        base = worker * per_worker

        @pl.loop(0, n_chunks)
        def _(j):
            off = pl.multiple_of(base + j * chunk, chunk)
            pltpu.sync_copy(idx_hbm.at[pl.ds(off, chunk)], idx_v)
            pltpu.async_copy(table_hbm.at[idx_v], rows_v, sem).wait()
            pltpu.sync_copy(rows_v, out_hbm.at[pl.ds(off, chunk)])

    return gather_kernel(table, idx)


def _combine_body(x_ref, ya_ref, yb_ref, route_ref, mod_ref):
    r = route_ref[...]
    ga, gb = r[:, TOP_K:TOP_K + 1], r[:, TOP_K + 1:TOP_K + 2]
    ya = _unpack_bf16_pair(ya_ref[...])
    yb = _unpack_bf16_pair(yb_ref[...])
    f = jnp.concatenate([ga * ya[0] + gb * yb[0], ga * ya[1] + gb * yb[1]], axis=1)
    return x_ref[...] + mod_ref[0][5:6, :] * f


def _combine_kernel(x_ref, ya_ref, yb_ref, route_ref, mod_ref, o_ref):
    o_ref[...] = _combine_body(x_ref, ya_ref, yb_ref, route_ref, mod_ref)


def _combine_norm_kernel(x_ref, ya_ref, yb_ref, route_ref, mod_ref, nw_ref, o_ref):
    o_ref[...] = _rms(_combine_body(x_ref, ya_ref, yb_ref, route_ref, mod_ref), nw_ref[...])


def _combine(x, y2, route, mod, seq_len, tm, final_nw=None):
    n, d = x.shape
    tps = seq_len // tm
    nt = n // tm
    in_specs = [
        pl.BlockSpec((tm, d), lambda i: (i, 0)),
        pl.BlockSpec((tm, d // 2), lambda i: (i, 0)),
        pl.BlockSpec((tm, d // 2), lambda i: (i + nt, 0)),
        pl.BlockSpec((tm, LANES), lambda i: (i, 0)),
        pl.BlockSpec((1, 6, d), lambda i: (i // tps, 0, 0)),
    ]
    args = [x, y2, y2, route, mod]
    body = _combine_kernel
    if final_nw is not None:
        in_specs.append(pl.BlockSpec((1, d), lambda i: (0, 0)))
        args.append(final_nw)
        body = _combine_norm_kernel
    return pl.pallas_call(
        body,
        grid=(n // tm,),
        in_specs=in_specs,
        out_specs=pl.BlockSpec((tm, d), lambda i: (i, 0)),
        out_shape=jax.ShapeDtypeStruct((n, d), F32),
        compiler_params=_cparams("arbitrary"),
        name="moe_combine",
    )(*args)


def _final_norm_kernel(x_ref, w_ref, o_ref):
    o_ref[...] = _rms(x_ref[...], w_ref[...])


def _final_norm(x, w, tm):
    n, d = x.shape
    return pl.pallas_call(
        _final_norm_kernel,
        grid=(n // tm,),
        in_specs=[pl.BlockSpec((tm, d), lambda i: (i, 0)), pl.BlockSpec((1, d), lambda i: (0, 0))],
        out_specs=pl.BlockSpec((tm, d), lambda i: (i, 0)),
        out_shape=jax.ShapeDtypeStruct((n, d), F32),
        compiler_params=_cparams("arbitrary"),
        name="final_norm",
    )(x, w)


def _moe(h, x, mod, route, w1, w3, w2, m, seq_len, tm, tf, tm_c, final_nw):
    n, d = x.shape
    n_asg = n * TOP_K
    flat_e = route[:, :TOP_K].astype(jnp.int32).reshape(-1)
    onehot = (flat_e[:, None] == jnp.arange(N_EXPERTS, dtype=jnp.int32)[None, :]).astype(jnp.int32)
    csum = jnp.cumsum(onehot, axis=0)
    counts = csum[-1]
    rank = jnp.take_along_axis(csum, flat_e[:, None], axis=1)[:, 0] - 1
    pcounts = (counts + tm - 1) // tm * tm
    pend = jnp.cumsum(pcounts)
    pstart = pend - pcounts
    dest = pstart[flat_e] + rank
    n_blocks = -(-n_asg // tm) + N_EXPERTS
    block_start = jnp.arange(n_blocks, dtype=pend.dtype) * tm
    block_e = jnp.minimum(jnp.sum((pend[None, :] <= block_start[:, None]).astype(jnp.int32), axis=1),
                          N_EXPERTS - 1)
    n_used = (pend[-1:] // tm).astype(jnp.int32)
    n_valid = jnp.clip((pstart + counts)[block_e] - block_start, 0, tm).astype(jnp.int32)
    block_e = jnp.where(jnp.arange(n_blocks) < n_used[0], block_e, block_e[jnp.maximum(n_used[0] - 1, 0)])
    block_e = block_e + m * N_EXPERTS
    dest_ab = dest.reshape(n, TOP_K).T
    xs = _scatter_rows(h, dest_ab[0], dest_ab[1], n_blocks * tm)
    ys = _expert_ffn(xs, block_e, n_used, n_valid, w1, w3, w2, tm, tf)
    y2 = _gather_rows(ys, dest_ab.reshape(-1))
    return _combine(x, y2, route, mod, seq_len, tm_c, final_nw)


def _prep_weights(w_in, gla_w_gk_up, gla_b_gk, hg_lb_logits):
    depth = w_in.shape[0]
    sizes = (HG_WIDTH,) * 5 + (GLA_QK, GLA_QK, GLA_WIDTH, GLA_WIDTH, GLA_GATE_RANK, GLA_GATE_RANK)
    offs = [0]
    for s in sizes:
        offs.append(offs[-1] + s)
    seg = lambda k: w_in[:, :, offs[k]:offs[k + 1]]
    hq, hf_f, hf_b, hi, hg, gq, gk, gv, gg, lr_f, lr_b = [seg(k) for k in range(11)]
    pad = jnp.zeros(w_in.shape[:2] + (LANES - 2 * GLA_GATE_RANK,), w_in.dtype)
    w_perm = jnp.concatenate([hq, hi, hg, gq, gk, gv, gg, hf_f, hf_b, lr_f, lr_b, pad], axis=-1).astype(BF16)

    r = GLA_GATE_RANK
    wup = jnp.zeros((depth, LANES, GLA_HEADS // 2, 2, LANES), F32)
    bup = jnp.zeros((depth, GLA_HEADS // 2, 2, LANES), F32)
    for dd in range(2):
        wup = wup.at[:, dd * r:(dd + 1) * r, :, dd, :].set(
            gla_w_gk_up[:, dd].reshape(depth, r, GLA_HEADS // 2, LANES))
        bup = bup.at[:, :, dd, :].set(gla_b_gk[:, dd].reshape(depth, GLA_HEADS // 2, LANES))
    wup = wup.reshape(depth, LANES, GLA_HEADS * LANES).astype(BF16)
    bup = bup.reshape(depth, 1, GLA_HEADS * LANES)

    lb = jnp.cumsum(jax.nn.softmax(hg_lb_logits.astype(F32), axis=0), axis=0)
    lb = lb - lb[0]
    return w_perm, wup, bup, lb


def _pick(pref, total):
    t = min(pref, total)
    while total % t:
        t //= 2
    return t


def _trunk(x3, mods, weights):
    (norm1_w, w_perm, lb, wup, bup, hg_norm_w, gla_norm_w, w_out, norm2_w, w_ff1, w_ff3, w_ff2,
     w_router, b_router, w_e1, w_e3, w_e2, final_norm_w) = weights
    nb, seq_len, d = x3.shape
    depth = norm1_w.shape[0]
    n = nb * seq_len
    x = x3.reshape(n, d)
    tm = _pick(1024, seq_len)
    tm_ffn = _pick(512, seq_len)
    tm_moe = _pick(512, n * TOP_K)
    for l in range(depth):
        mod = mods[l]
        pa, pk, plog = _in_proj(x, mod, norm1_w[l][None, :], w_perm, l, lb[l], wup[l], bup[l], seq_len,
                                _pick(1024, seq_len))
        mh = _hg_mixer(pa, pk, plog, hg_norm_w[l][None, :], seq_len)
        mg = _gla_mixer(pa, plog, gla_norm_w[l][None, :], seq_len)
        m = l // 2
        if l % 2 == 0:
            x = _out_dense(x, mh, mg, mod, w_out, l, norm2_w[l][None, :], w_ff1, w_ff3, w_ff2, m, seq_len,
                           tm_ffn)
        else:
            wr = jnp.pad(w_router[m], ((0, 0), (0, LANES - N_EXPERTS)))
            br = jnp.pad(b_router[m], (0, LANES - N_EXPERTS))[None, :]
            x, h, route = _out_proj(x, mh, mg, mod, w_out, l, norm2_w[l][None, :], seq_len, tm,
                                    router=(wr, br))
            final_nw = final_norm_w[None, :] if l == depth - 1 else None
            x = _moe(h, x, mod, route, w_e1, w_e3, w_e2, m, seq_len, tm_moe,
                     w_e1.shape[-1], tm, final_nw)
    if depth % 2 == 1:
        x = _final_norm(x, final_norm_w[None, :], tm)
    return x.reshape(nb, seq_len, d)


def kernel(x_prompt, x_sample, c_prompt, c_sample, w_ada, b_ada, norm1_w, w_in, hg_lb_logits, gla_w_gk_up, gla_b_gk, hg_norm_w, gla_norm_w, w_out, norm2_w, w_ff1, w_ff3, w_ff2, w_router, b_router, w_e1, w_e3, w_e2, final_norm_w):
    depth, d = norm1_w.shape
    w_perm, wup, bup, lb = _prep_weights(w_in, gla_w_gk_up, gla_b_gk, hg_lb_logits)
    nbp = c_prompt.shape[0]
    mods = _ada_mods(jnp.concatenate([c_prompt, c_sample], axis=0), w_ada, b_ada)
    mods = mods.reshape(depth, mods.shape[1], 6, d)
    stack = lambda w: w.astype(BF16).reshape((-1,) + w.shape[2:])
    weights = (norm1_w, w_perm, lb, wup, bup, hg_norm_w, gla_norm_w, w_out.astype(BF16), norm2_w,
               w_ff1.astype(BF16), w_ff3.astype(BF16), w_ff2.astype(BF16), w_router, b_router,
               stack(w_e1), stack(w_e3), stack(w_e2), final_norm_w)
    y_prompt = _trunk(x_prompt, mods[:, :nbp], weights)
    y_sample = _trunk(x_sample, mods[:, nbp:], weights)
    return (y_prompt, y_sample)
```
